```python
import jax, jax.numpy as jnp
from jax import lax
import numpy as np

D_MODEL = 2048
BATCH = 4
SEQ = 2048
DEPTH = 4

N_MEM = 256
HG_HEADS = 8
HG_DK = 128
HG_DV = 128
HG_KEY_WIDTH = HG_HEADS * HG_DK
HG_WIDTH = HG_HEADS * HG_DV
HG_CHUNK = 64
LRU_WIDTH = 1024
LRU_BLOCKS = 8
LRU_BLOCK = LRU_WIDTH // LRU_BLOCKS
CONV_WIDTH = 4
LRU_C = 8.0
XA_HEADS = 4
XA_HEAD_DIM = 256
XA_WIDTH = XA_HEADS * XA_HEAD_DIM
MIX_WIDTH = HG_WIDTH + LRU_WIDTH + XA_WIDTH
IN_WIDTH = 2 * HG_KEY_WIDTH + 2 * HG_WIDTH + 2 * LRU_WIDTH + 2 * XA_WIDTH
EPS = 1e-6

kernel_name = 'hymba_style_hgrn2_rglru_memxattn'


def rmsnorm(x, g):
    xf = x.astype(jnp.float32)
    return xf * lax.rsqrt(jnp.mean(xf * xf, axis=-1, keepdims=True) + EPS) * g.astype(jnp.float32)


def split_columns(proj):
    sizes = [HG_KEY_WIDTH, HG_KEY_WIDTH, HG_WIDTH, HG_WIDTH, LRU_WIDTH, LRU_WIDTH, XA_WIDTH, XA_WIDTH]
    points = np.cumsum(sizes)[:-1].tolist()
    return jnp.split(proj, points, axis=-1)


def hgrn2(q, f_pre, v, lb):
    b, s, _ = q.shape
    n = s // HG_CHUNK
    f = lb + (1.0 - lb) * jax.nn.sigmoid(f_pre.astype(jnp.float32))
    log_f = jnp.log(f)
    k = 1.0 - f

    def chunks(t, d):
        return t.astype(jnp.float32).reshape(b, n, HG_CHUNK, HG_HEADS, d).transpose(1, 0, 3, 2, 4)

    qc, kc, gc, vc = chunks(q, HG_DK), chunks(k, HG_DK), chunks(log_f, HG_DK), chunks(v, HG_DV)
    causal = jnp.tril(jnp.ones((HG_CHUNK, HG_CHUNK), dtype=bool))[:, :, None]

    def step(state, inp):
        q_, k_, g_, v_ = inp
        cum = jnp.cumsum(g_, axis=2)
        o_inter = jnp.einsum('bhtk,bhkv->bhtv', q_ * jnp.exp(cum), state)
        diff = cum[:, :, :, None, :] - cum[:, :, None, :, :]
        decay = jnp.where(causal, jnp.exp(jnp.where(causal, diff, 0.0)), 0.0)
        scores = jnp.einsum('bhtk,bhsk,bhtsk->bhts', q_, k_, decay)
        o = o_inter + jnp.einsum('bhts,bhsv->bhtv', scores, v_)
        last = cum[:, :, -1:, :]
        state = (jnp.exp(last[:, :, 0, :])[..., None] * state
                 + jnp.einsum('bhsk,bhsv->bhkv', k_ * jnp.exp(last - cum), v_))
        return state, o

    state0 = jnp.zeros((b, HG_HEADS, HG_DK, HG_DV), jnp.float32)
    _, o = lax.scan(step, state0, (qc, kc, gc, vc))
    return o.transpose(1, 0, 3, 2, 4).reshape(b, s, HG_HEADS, HG_DV)


def rglru(xb, conv_w, conv_b, w_r, b_r, w_i, b_i, lam):
    b, s, _ = xb.shape
    xc = lax.conv_general_dilated(
        xb, conv_w[:, None, :].astype(xb.dtype), window_strides=(1,),
        padding=[(CONV_WIDTH - 1, 0)], dimension_numbers=('NWC', 'WIO', 'NWC'),
        feature_group_count=LRU_WIDTH)
    xc = xc.astype(jnp.float32) + conv_b.astype(jnp.float32)
    xblk = xc.reshape(b, s, LRU_BLOCKS, LRU_BLOCK)
    r = jax.nn.sigmoid(jnp.einsum('bsnc,ncd->bsnd', xblk, w_r.astype(jnp.float32))
                       + b_r.astype(jnp.float32)).reshape(b, s, LRU_WIDTH)
    i = jax.nn.sigmoid(jnp.einsum('bsnc,ncd->bsnd', xblk, w_i.astype(jnp.float32))
                       + b_i.astype(jnp.float32)).reshape(b, s, LRU_WIDTH)
    log_a = -LRU_C * r * jax.nn.softplus(-lam.astype(jnp.float32))
    a = jnp.exp(log_a)
    u = jnp.sqrt(-jnp.expm1(2.0 * log_a)) * (i * xc)

    def combine(left, right):
        a_l, u_l = left
        a_r, u_r = right
        return a_l * a_r, a_r * u_l + u_r

    _, h = lax.associative_scan(combine, (a, u), axis=1)
    return h


def mem_attend(q, mem_n, w_kv):
    b, s, _ = q.shape
    m = mem_n.shape[1]
    kv = jnp.einsum('bmd,de->bme', mem_n, w_kv)
    k, v = jnp.split(kv, 2, axis=-1)
    qh = q.reshape(b, s, XA_HEADS, XA_HEAD_DIM).astype(jnp.float32)
    kh = k.reshape(b, m, XA_HEADS, XA_HEAD_DIM).astype(jnp.float32)
    vh = v.reshape(b, m, XA_HEADS, XA_HEAD_DIM).astype(jnp.float32)
    scores = jnp.einsum('bshd,bmhd->bhsm', qh, kh) * (XA_HEAD_DIM ** -0.5)
    p = jax.nn.softmax(scores, axis=-1)
    return jnp.einsum('bhsm,bmhd->bshd', p, vh).reshape(b, s, XA_WIDTH)


def setup_inputs(seed: int = 0) -> dict:
    key = jax.random.key(seed)
    ks = jax.random.split(key, 20)
    f32 = jnp.float32
    a0 = jax.random.uniform(ks[12], (DEPTH, LRU_WIDTH), f32, 0.9, 0.999)
    s0 = a0 ** (1.0 / LRU_C)
    lam = jnp.log(s0) - jnp.log1p(-s0)
    return {
        'x': jax.random.normal(ks[0], (BATCH, SEQ, D_MODEL), f32),
        'mem': jax.random.normal(ks[1], (BATCH, N_MEM, D_MODEL), f32),
        'norm_g': 1.0 + 0.02 * jax.random.normal(ks[2], (DEPTH, D_MODEL), f32),
        'w_in': jax.random.normal(ks[3], (DEPTH, D_MODEL, IN_WIDTH), f32) * D_MODEL ** -0.5,
        'lb_param': 0.1 * jax.random.normal(ks[4], (DEPTH, HG_KEY_WIDTH), f32),
        'hg_norm_g': 1.0 + 0.02 * jax.random.normal(ks[5], (DEPTH, HG_WIDTH), f32),
        'conv_w': jax.random.normal(ks[6], (DEPTH, CONV_WIDTH, LRU_WIDTH), f32) * CONV_WIDTH ** -0.5,
        'conv_b': 0.01 * jax.random.normal(ks[7], (DEPTH, LRU_WIDTH), f32),
        'w_r': jax.random.normal(ks[8], (DEPTH, LRU_BLOCKS, LRU_BLOCK, LRU_BLOCK), f32) * LRU_BLOCK ** -0.5,
        'b_r': 0.01 * jax.random.normal(ks[9], (DEPTH, LRU_BLOCKS, LRU_BLOCK), f32),
        'w_i': jax.random.normal(ks[10], (DEPTH, LRU_BLOCKS, LRU_BLOCK, LRU_BLOCK), f32) * LRU_BLOCK ** -0.5,
        'b_i': 0.01 * jax.random.normal(ks[11], (DEPTH, LRU_BLOCKS, LRU_BLOCK), f32),
        'lam': lam,
        'mem_norm_g': 1.0 + 0.02 * jax.random.normal(ks[13], (DEPTH, D_MODEL), f32),
        'w_kv': jax.random.normal(ks[14], (DEPTH, D_MODEL, 2 * XA_WIDTH), f32) * D_MODEL ** -0.5,
        'w_out': jax.random.normal(ks[15], (DEPTH, MIX_WIDTH, D_MODEL), f32) * MIX_WIDTH ** -0.5,
        'final_g': 1.0 + 0.02 * jax.random.normal(ks[16], (D_MODEL,), f32),
    }


def reference(x, mem, norm_g, w_in, lb_param, hg_norm_g, conv_w, conv_b, w_r, b_r, w_i, b_i,
              lam, mem_norm_g, w_kv, w_out, final_g):
    b, s, _ = x.shape
    lbs = jnp.cumsum(jax.nn.softmax(lb_param.astype(jnp.float32), axis=0), axis=0)
    lbs = lbs - lbs[0:1]
    for l in range(DEPTH):
        h = rmsnorm(x, norm_g[l]).astype(x.dtype)
        proj = jnp.einsum('bsd,de->bse', h, w_in[l])
        q_a, f_a, i_a, g_a, x_b, g_b, q_c, g_c = split_columns(proj)
        o_a = hgrn2(q_a, f_a, i_a, lbs[l])
        o_a = rmsnorm(o_a, hg_norm_g[l].reshape(HG_HEADS, HG_DV)).reshape(b, s, HG_WIDTH)
        o_a = o_a * jax.nn.silu(g_a.astype(jnp.float32))
        o_b = rglru(x_b, conv_w[l], conv_b[l], w_r[l], b_r[l], w_i[l], b_i[l], lam[l])
        o_b = o_b * jax.nn.silu(g_b.astype(jnp.float32))
        mem_n = rmsnorm(mem, mem_norm_g[l]).astype(w_kv.dtype)
        o_c = mem_attend(q_c, mem_n, w_kv[l]) * jax.nn.silu(g_c.astype(jnp.float32))
        mixed = jnp.concatenate([o_a, o_b, o_c], axis=-1).astype(x.dtype)
        x = x + jnp.einsum('bse,ed->bsd', mixed, w_out[l]).astype(x.dtype)
    return rmsnorm(x, final_g).astype(x.dtype)
```

```python
import functools

import numpy as np
import jax
import jax.numpy as jnp
from jax import lax
from jax.experimental import pallas as pl
from jax.experimental.pallas import tpu as pltpu

F32 = jnp.float32
BF16 = jnp.bfloat16

EPS = 1e-6
LRU_C = 8.0
CONV_WIDTH = 4

HG_HEADS = 8
HG_DK = 128
HG_DV = 128
LRU_BLOCKS = 8
LRU_BLOCK = 128
XA_HEADS = 4
XA_HEAD_DIM = 256

SUBLANES = 8
LANES = 128
VMEM_LIMIT_BYTES = 56 * 1024 * 1024

HG_CHUNK = 64
HG_DIAG = SUBLANES
LRU_ROWS = 64


def _sigmoid(x):
    return 1.0 / (1.0 + jnp.exp(-x))


def _silu(x):
    return x * _sigmoid(x)


def _inproj_kernel(x_ref, g_ref, w_ref, o_ref, h_ref):
    @pl.when(pl.program_id(1) == 0)
    def _():
        x = x_ref[...]
        ms = jnp.mean(x * x, axis=-1, keepdims=True)
        h_ref[...] = (x * lax.rsqrt(ms + EPS) * g_ref[...]).astype(BF16)

    o_ref[...] = jnp.dot(h_ref[...], w_ref[...], preferred_element_type=F32)


def _inproj(x2, g, w_bf16, tm=1024, tn=512):
    t, d = x2.shape
    n = w_bf16.shape[1]
    return pl.pallas_call(
        _inproj_kernel,
        grid=(t // tm, n // tn),
        in_specs=[
            pl.BlockSpec((tm, d), lambda i, j: (i, 0)),
            pl.BlockSpec((1, d), lambda i, j: (0, 0)),
            pl.BlockSpec((d, tn), lambda i, j: (0, j)),
        ],
        out_specs=pl.BlockSpec((tm, tn), lambda i, j: (i, j)),
        out_shape=jax.ShapeDtypeStruct((t, n), F32),
        scratch_shapes=[pltpu.VMEM((tm, d), BF16)],
        compiler_params=pltpu.CompilerParams(
            dimension_semantics=("parallel", "arbitrary"),
            vmem_limit_bytes=VMEM_LIMIT_BYTES),
        name="inproj",
    )(x2, g, w_bf16)


def _hgrn_tables():
    c = HG_CHUNK
    t = np.arange(c)[:, None]
    j = np.arange(c)[None, :]
    sums = [j <= t, j > t]
    masks = []
    w = c // 2
    while w >= HG_DIAG:
        p = t % (2 * w)
        mid = t - p + w - 1
        second = p >= w
        sums.append(np.where(second, (j > mid) & (j <= t), (j > t) & (j <= mid)))
        same_group = (t // (2 * w)) == (j // (2 * w))
        masks.append(same_group & second & ((j % (2 * w)) < w))
        w //= 2
    return (np.concatenate(sums, 0).astype(np.float32),
            np.stack(masks).astype(np.float32))


def _split3_bf16(x):
    hi = x.astype(BF16)
    r1 = x - hi.astype(F32)
    mid = r1.astype(BF16)
    lo = (r1 - mid.astype(F32)).astype(BF16)
    return hi, mid, lo


_NT = (((1,), (1,)), ((), ()))
_TN = (((0,), (0,)), ((), ()))


def _hgrn_kernel(q_ref, f_ref, v_ref, gate_ref, lbp_ref, ng_ref, sums_ref, masks_ref,
                 o_ref, st_ref, *, layer):
    c = HG_CHUNK
    n_levels = masks_ref.shape[0]
    n_chunks = q_ref.shape[0] // c
    lbp = lbp_ref[...]
    lbe = jnp.exp(lbp - jnp.max(lbp, axis=0, keepdims=True))
    sm = lbe / jnp.sum(lbe, axis=0, keepdims=True)
    lb = jnp.sum(sm[1:layer + 1], axis=0, keepdims=True) if layer else jnp.zeros((1, HG_DK), F32)
    ng = ng_ref[...]
    st_ref[...] = jnp.zeros_like(st_ref)
    row = lax.broadcasted_iota(jnp.int32, (c, LANES), 0)

    def chunk(n, carry):
        r0 = pl.multiple_of(n * c, c)
        q = q_ref[pl.ds(r0, c), :]
        f = lb + (1.0 - lb) * _sigmoid(f_ref[pl.ds(r0, c), :])
        kk = 1.0 - f
        v = v_ref[pl.ds(r0, c), :]
        g = jnp.log(f)

        g3 = jnp.concatenate(_split3_bf16(g), axis=1)
        e3 = jnp.dot(sums_ref[...], g3, preferred_element_type=F32)
        e = e3[:, :HG_DK] + e3[:, HG_DK:2 * HG_DK] + e3[:, 2 * HG_DK:]
        cum = e[0:c]
        suffix = e[c:2 * c]

        st = st_ref[...]
        qi = (q * jnp.exp(cum)).astype(BF16)
        o = lax.dot_general(qi, st.astype(BF16), _NT, preferred_element_type=F32)

        scores = jnp.zeros((c, c), F32)
        w = c // 2
        for lvl in range(n_levels):
            second = (row & w) != 0
            x = (jnp.where(second, q, kk) * jnp.exp(e[(2 + lvl) * c:(3 + lvl) * c])).astype(BF16)
            a = lax.dot_general(x, x, _NT, preferred_element_type=F32)
            scores = scores + a * masks_ref[lvl]
            w //= 2
        o = o + jnp.dot(scores.astype(BF16), v.astype(BF16), preferred_element_type=F32)

        rin = row & (HG_DIAG - 1)
        sc = jnp.sum(q * kk, axis=-1, keepdims=True)
        o = o + sc * v
        for d in range(1, HG_DIAG):
            valid = rin >= d
            kd = pltpu.roll(kk, d, 0)
            cd = pltpu.roll(cum, d, 0)
            vd = pltpu.roll(v, d, 0)
            p = jnp.where(valid, q * kd * jnp.exp(jnp.minimum(cum - cd, 0.0)), 0.0)
            sc = jnp.sum(p, axis=-1, keepdims=True)
            o = o + sc * vd

        ks = (kk * jnp.exp(suffix)).astype(BF16)
        st_ref[...] = (st * jnp.exp(cum[c - 1:c, :])
                       + lax.dot_general(v.astype(BF16), ks, _TN, preferred_element_type=F32))

        ms = jnp.mean(o * o, axis=-1, keepdims=True)
        y = o * lax.rsqrt(ms + EPS) * ng * _silu(gate_ref[pl.ds(r0, c), :])
        o_ref[pl.ds(r0, c), :] = y.astype(o_ref.dtype)
        return carry

    lax.fori_loop(0, n_chunks, chunk, 0)


def _hgrn(proj, lb_param, ng, layer, batch, seq):
    sums, masks = _hgrn_tables()
    t = proj.shape[0]
    nh = HG_HEADS
    depth = lb_param.shape[0]

    def col(base):
        return pl.BlockSpec((seq, LANES), lambda b, h: (b, base + h))

    def const(shape):
        return pl.BlockSpec(shape, lambda b, h: (0,) * len(shape))

    return pl.pallas_call(
        functools.partial(_hgrn_kernel, layer=layer),
        grid=(batch, nh),
        in_specs=[
            col(0), col(nh), col(2 * nh), col(3 * nh),
            pl.BlockSpec((depth, LANES), lambda b, h: (0, h)),
            pl.BlockSpec((1, LANES), lambda b, h: (0, h)),
            const(sums.shape), const(masks.shape),
        ],
        out_specs=pl.BlockSpec((seq, LANES), lambda b, h: (b, h)),
        out_shape=jax.ShapeDtypeStruct((t, nh * HG_DV), BF16),
        scratch_shapes=[pltpu.VMEM((HG_DV, HG_DK), F32)],
        compiler_params=pltpu.CompilerParams(
            dimension_semantics=("parallel", "parallel"),
            vmem_limit_bytes=VMEM_LIMIT_BYTES),
        name="hgrn2",
    )(proj, proj, proj, proj, lb_param, ng, jnp.asarray(sums, BF16), jnp.asarray(masks))


def _lru_kernel(x_ref, gate_ref, cw_ref, cb_ref, wri_ref, bri_ref, lam_ref, o_ref):
    rws = LRU_ROWS
    n_steps = x_ref.shape[0] // rws
    nv = rws // SUBLANES
    cw = cw_ref[...]
    cb = cb_ref[...]
    bri = bri_ref[...]
    sp = jnp.log1p(jnp.exp(-jnp.abs(lam_ref[...]))) + jnp.maximum(-lam_ref[...], 0.0)
    row8 = lax.broadcasted_iota(jnp.int32, (SUBLANES, LANES), 0)
    rin = lax.broadcasted_iota(jnp.int32, (rws, LANES), 0) & (SUBLANES - 1)

    def step(n, carry):
        prev8, hlast = carry
        r0 = pl.multiple_of(n * rws, rws)
        x = x_ref[pl.ds(r0, rws), :]

        xc = x * cw[CONV_WIDTH - 1:CONV_WIDTH, :] + cb
        for s in range(1, CONV_WIDTH):
            rolled = pltpu.roll(x, s, 0)
            top = jnp.where(row8 < s, pltpu.roll(prev8, s, 0), rolled[:SUBLANES])
            xs = jnp.concatenate([top, rolled[SUBLANES:]], axis=0)
            xc = xc + xs * cw[CONV_WIDTH - 1 - s:CONV_WIDTH - s, :]

        ri = jnp.dot(xc.astype(BF16), wri_ref[0], preferred_element_type=F32) + bri
        r = _sigmoid(ri[:, :LANES])
        i = _sigmoid(ri[:, LANES:])
        log_a = (-LRU_C) * r * sp
        a = jnp.exp(log_a)
        u = jnp.sqrt(-jnp.tanh(log_a) * (a * a + 1.0)) * (i * xc)

        for k in (1, 2, 4):
            valid = rin >= k
            u = u + a * jnp.where(valid, pltpu.roll(u, k, 0), 0.0)
            a = a * jnp.where(valid, pltpu.roll(a, k, 0), 1.0)
        hs = []
        for i8 in range(nv):
            sl = slice(i8 * SUBLANES, (i8 + 1) * SUBLANES)
            h8 = a[sl] * hlast + u[sl]
            hlast = jnp.broadcast_to(h8[SUBLANES - 1:SUBLANES, :], (SUBLANES, LANES))
            hs.append(h8)
        h = jnp.concatenate(hs, axis=0)

        y = h * _silu(gate_ref[pl.ds(r0, rws), :])
        o_ref[pl.ds(r0, rws), :] = y.astype(o_ref.dtype)
        return x[rws - SUBLANES:, :], hlast

    zeros8 = jnp.zeros((SUBLANES, LANES), F32)
    lax.fori_loop(0, n_steps, step, (zeros8, zeros8))


def _lru(proj, cw, cb, wri, bri, lam, batch, seq, x_col, gate_col):
    t = proj.shape[0]
    nb = LRU_BLOCKS

    def col(base):
        return pl.BlockSpec((seq, LANES), lambda b, n: (b, base + n))

    def vec(rows):
        return pl.BlockSpec((rows, LANES), lambda b, n: (0, n))

    return pl.pallas_call(
        _lru_kernel,
        grid=(batch, nb),
        in_specs=[
            col(x_col), col(gate_col),
            vec(CONV_WIDTH), vec(1),
            pl.BlockSpec((1, LRU_BLOCK, 2 * LRU_BLOCK), lambda b, n: (n, 0, 0)),
            pl.BlockSpec((1, 2 * LANES), lambda b, n: (0, n)),
            vec(1),
        ],
        out_specs=pl.BlockSpec((seq, LANES), lambda b, n: (b, n)),
        out_shape=jax.ShapeDtypeStruct((t, nb * LRU_BLOCK), BF16),
        compiler_params=pltpu.CompilerParams(
            dimension_semantics=("parallel", "parallel"),
            vmem_limit_bytes=VMEM_LIMIT_BYTES),
        name="rglru",
    )(proj, proj, cw, cb, wri, bri, lam)


def _kv_kernel(mem_ref, g_ref, w_ref, o_ref, h_ref):
    @pl.when(pl.program_id(1) == 0)
    def _():
        x = mem_ref[...]
        ms = jnp.mean(x * x, axis=-1, keepdims=True)
        h_ref[...] = (x * lax.rsqrt(ms + EPS) * g_ref[0]).astype(BF16)

    o_ref[0] = jnp.dot(h_ref[...], w_ref[0], preferred_element_type=F32).astype(o_ref.dtype)


def _kv(mem2, g, w_bf16, tn=512):
    depth, d, n = w_bf16.shape
    m = mem2.shape[0]
    return pl.pallas_call(
        _kv_kernel,
        grid=(depth, n // tn),
        in_specs=[
            pl.BlockSpec((m, d), lambda l, j: (0, 0)),
            pl.BlockSpec((1, 1, d), lambda l, j: (l, 0, 0)),
            pl.BlockSpec((1, d, tn), lambda l, j: (l, 0, j)),
        ],
        out_specs=pl.BlockSpec((1, m, tn), lambda l, j: (l, 0, j)),
        out_shape=jax.ShapeDtypeStruct((depth, m, n), BF16),
        scratch_shapes=[pltpu.VMEM((m, d), BF16)],
        compiler_params=pltpu.CompilerParams(
            dimension_semantics=("parallel", "arbitrary"),
            vmem_limit_bytes=VMEM_LIMIT_BYTES),
        name="mem_kv",
    )(mem2, g, w_bf16)


def _attn_kernel(q_ref, gate_ref, k_ref, v_ref, o_ref):
    scale = XA_HEAD_DIM ** -0.5
    for h in range(XA_HEADS):
        sl = slice(h * XA_HEAD_DIM, (h + 1) * XA_HEAD_DIM)
        s = lax.dot_general(q_ref[:, sl].astype(BF16), k_ref[0, :, sl], _NT,
                            preferred_element_type=F32) * scale
        m = jnp.max(s, axis=-1, keepdims=True)
        p = jnp.exp(s - m)
        p = p / jnp.sum(p, axis=-1, keepdims=True)
        o = jnp.dot(p.astype(BF16), v_ref[0, :, sl], preferred_element_type=F32)
        o_ref[:, sl] = (o * _silu(gate_ref[:, sl])).astype(o_ref.dtype)


def _attn(proj, kv, layer, batch, seq, n_mem, q_col, gate_col, ts=512):
    t = proj.shape[0]
    width = XA_HEADS * XA_HEAD_DIM
    nt = seq // ts
    return pl.pallas_call(
        _attn_kernel,
        grid=(batch, nt),
        in_specs=[
            pl.BlockSpec((ts, width), lambda b, i: (b * nt + i, q_col)),
            pl.BlockSpec((ts, width), lambda b, i: (b * nt + i, gate_col)),
            pl.BlockSpec((1, n_mem, width), lambda b, i: (layer * batch + b, 0, 0)),
            pl.BlockSpec((1, n_mem, width), lambda b, i: (layer * batch + b, 0, 1)),
        ],
        out_specs=pl.BlockSpec((ts, width), lambda b, i: (b * nt + i, 0)),
        out_shape=jax.ShapeDtypeStruct((t, width), BF16),
        compiler_params=pltpu.CompilerParams(
            dimension_semantics=("parallel", "parallel"),
            vmem_limit_bytes=VMEM_LIMIT_BYTES),
        name="mem_attn",
    )(proj, proj, kv, kv)


def _outproj_kernel(x_ref, a_ref, b_ref, c_ref, w_ref, o_ref):
    ka = a_ref.shape[1]
    kb = b_ref.shape[1]
    acc = x_ref[...]
    acc = acc + jnp.dot(a_ref[...], w_ref[0:ka, :], preferred_element_type=F32)
    acc = acc + jnp.dot(b_ref[...], w_ref[ka:ka + kb, :], preferred_element_type=F32)
    acc = acc + jnp.dot(c_ref[...], w_ref[ka + kb:, :], preferred_element_type=F32)
    o_ref[...] = acc


def _outproj(x2, oa, ob, oc, w_bf16, tm=1024, tn=512):
    t, d = x2.shape
    k = w_bf16.shape[0]

    def lhs(arr):
        return pl.BlockSpec((tm, arr.shape[1]), lambda i, j: (i, 0))

    return pl.pallas_call(
        _outproj_kernel,
        grid=(t // tm, d // tn),
        in_specs=[
            pl.BlockSpec((tm, tn), lambda i, j: (i, j)),
            lhs(oa), lhs(ob), lhs(oc),
            pl.BlockSpec((k, tn), lambda i, j: (0, j)),
        ],
        out_specs=pl.BlockSpec((tm, tn), lambda i, j: (i, j)),
        out_shape=jax.ShapeDtypeStruct((t, d), F32),
        compiler_params=pltpu.CompilerParams(
            dimension_semantics=("parallel", "arbitrary"),
            vmem_limit_bytes=VMEM_LIMIT_BYTES),
        name="outproj",
    )(x2, oa, ob, oc, w_bf16)


def _norm_kernel(x_ref, g_ref, o_ref):
    x = x_ref[...]
    ms = jnp.mean(x * x, axis=-1, keepdims=True)
    o_ref[...] = x * lax.rsqrt(ms + EPS) * g_ref[...]


def _final_norm(x2, g, tm=512):
    t, d = x2.shape
    return pl.pallas_call(
        _norm_kernel,
        grid=(t // tm,),
        in_specs=[pl.BlockSpec((tm, d), lambda i: (i, 0)),
                  pl.BlockSpec((1, d), lambda i: (0, 0))],
        out_specs=pl.BlockSpec((tm, d), lambda i: (i, 0)),
        out_shape=jax.ShapeDtypeStruct((t, d), F32),
        compiler_params=pltpu.CompilerParams(
            dimension_semantics=("parallel",),
            vmem_limit_bytes=VMEM_LIMIT_BYTES),
        name="final_norm",
    )(x2, g)


def kernel(x, mem, norm_g, w_in, lb_param, hg_norm_g, conv_w, conv_b, w_r, b_r, w_i, b_i, lam,
           mem_norm_g, w_kv, w_out, final_g):
    batch, seq, d_model = x.shape
    depth = w_in.shape[0]
    n_mem = mem.shape[1]
    hg_key = HG_HEADS * HG_DK
    hg_val = HG_HEADS * HG_DV
    lru_w = LRU_BLOCKS * LRU_BLOCK
    xa_w = XA_HEADS * XA_HEAD_DIM
    assert seq % HG_CHUNK == 0 and seq % LRU_ROWS == 0
    assert w_in.shape[2] == 2 * hg_key + 2 * hg_val + 2 * lru_w + 2 * xa_w
    assert hg_key == hg_val == lru_w == xa_w
    lru_x_col = (2 * hg_key + 2 * hg_val) // LANES
    lru_gate_col = lru_x_col + lru_w // LANES
    xa_q_col = (2 * hg_key + 2 * hg_val + 2 * lru_w) // xa_w
    xa_gate_col = xa_q_col + 1

    w_in_b = w_in.astype(BF16)
    w_out_b = w_out.astype(BF16)
    w_kv_b = w_kv.astype(BF16)
    wri = jnp.concatenate([w_r, w_i], axis=-1).astype(BF16)
    bri = jnp.concatenate([b_r, b_i], axis=-1).reshape(depth, 1, -1)

    x2 = x.reshape(batch * seq, d_model)
    kv = _kv(mem.reshape(batch * n_mem, d_model), mem_norm_g.reshape(depth, 1, d_model), w_kv_b)
    kv = kv.reshape(depth * batch, n_mem, 2 * xa_w)

    for l in range(depth):
        proj = _inproj(x2, norm_g[l].reshape(1, d_model), w_in_b[l])
        o_a = _hgrn(proj, lb_param, hg_norm_g[l].reshape(1, hg_val), l, batch, seq)
        o_b = _lru(proj, conv_w[l], conv_b[l].reshape(1, lru_w), wri[l], bri[l],
                   lam[l].reshape(1, lru_w), batch, seq, lru_x_col, lru_gate_col)
        o_c = _attn(proj, kv, l, batch, seq, n_mem, xa_q_col, xa_gate_col)
        x2 = _outproj(x2, o_a, o_b, o_c, w_out_b[l])

    return _final_norm(x2, final_g.reshape(1, d_model)).reshape(batch, seq, d_model)
```

```python
import functools

import numpy as np
import jax
import jax.numpy as jnp
from jax import lax
from jax.experimental import pallas as pl
from jax.experimental.pallas import tpu as pltpu

F32 = jnp.float32
BF16 = jnp.bfloat16

EPS = 1e-6
LRU_C = 8.0
CONV_WIDTH = 4

HG_HEADS = 8
HG_DK = 128
HG_DV = 128
LRU_BLOCKS = 8
LRU_BLOCK = 128
XA_HEADS = 4
XA_HEAD_DIM = 256

SUBLANES = 8
LANES = 128
VMEM_LIMIT_BYTES = 56 * 1024 * 1024

HG_CHUNK = 64
HG_UNROLL = 4
LRU_ROWS = 256


def _sigmoid(x):
    return 1.0 / (1.0 + jnp.exp(-x))


def _silu(x):
    return x * _sigmoid(x)


def _inproj_kernel(x_ref, g_ref, w_ref, o_ref, h_ref):
    @pl.when(pl.program_id(1) == 0)
    def _():
        x = x_ref[...]
        ms = jnp.mean(x * x, axis=-1, keepdims=True)
        h_ref[...] = (x * lax.rsqrt(ms + EPS) * g_ref[...]).astype(BF16)

    o_ref[...] = jnp.dot(h_ref[...], w_ref[...], preferred_element_type=F32)


def _inproj(x2, g, w_bf16, tm=1024, tn=512):
    t, d = x2.shape
    n = w_bf16.shape[1]
    return pl.pallas_call(
        _inproj_kernel,
        grid=(t // tm, n // tn),
        in_specs=[
            pl.BlockSpec((tm, d), lambda i, j: (i, 0)),
            pl.BlockSpec((1, d), lambda i, j: (0, 0)),
            pl.BlockSpec((d, tn), lambda i, j: (0, j)),
        ],
        out_specs=pl.BlockSpec((tm, tn), lambda i, j: (i, j)),
        out_shape=jax.ShapeDtypeStruct((t, n), F32),
        scratch_shapes=[pltpu.VMEM((tm, d), BF16)],
        compiler_params=pltpu.CompilerParams(
            dimension_semantics=("parallel", "arbitrary"),
            vmem_limit_bytes=VMEM_LIMIT_BYTES),
        name="inproj",
    )(x2, g, w_bf16)


def _hgrn_masks():
    c = HG_CHUNK
    t = np.arange(c)[:, None]
    s = np.arange(c)[None, :]
    masks = []
    w = c // 2
    while w >= 1:
        same_group = (t // (2 * w)) == (s // (2 * w))
        masks.append(same_group & ((t % (2 * w)) >= w) & ((s % (2 * w)) < w))
        w //= 2
    return np.stack(masks).astype(np.float32)


def _bcast_row(x, r, rows):
    return jnp.broadcast_to(x[r:r + 1, :], (rows, x.shape[1]))


def _cumsum_rows(g, rin):
    for k in (1, 2, 4):
        g = g + jnp.where(rin >= k, pltpu.roll(g, k, 0), 0.0)
    tiles = [g[0:SUBLANES]]
    for i in range(1, g.shape[0] // SUBLANES):
        tiles.append(g[i * SUBLANES:(i + 1) * SUBLANES] + _bcast_row(tiles[-1], SUBLANES - 1, SUBLANES))
    return jnp.concatenate(tiles, axis=0)


def _group_mid(cum, w):
    c = cum.shape[0]
    if w >= SUBLANES:
        return jnp.concatenate([_bcast_row(cum, base + w - 1, 2 * w) for base in range(0, c, 2 * w)],
                               axis=0)
    tiles = []
    for i in range(c // SUBLANES):
        tile = cum[i * SUBLANES:(i + 1) * SUBLANES]
        if w == 4:
            tiles.append(_bcast_row(tile, 3, SUBLANES))
        else:
            tiles.append(jnp.concatenate([_bcast_row(tile, 1, 4), _bcast_row(tile, 5, 4)], axis=0))
    return jnp.concatenate(tiles, axis=0)


_NT = (((1,), (1,)), ((), ()))
_TN = (((0,), (0,)), ((), ()))


def _hgrn_kernel(q_ref, f_ref, v_ref, gate_ref, lbp_ref, ng_ref, masks_ref, o_ref, st_ref, *, layer):
    c = HG_CHUNK
    n_levels = masks_ref.shape[0]
    n_chunks = q_ref.shape[0] // c
    lbp = lbp_ref[...]
    lbe = jnp.exp(lbp - jnp.max(lbp, axis=0, keepdims=True))
    sm = lbe / jnp.sum(lbe, axis=0, keepdims=True)
    lb = jnp.sum(sm[1:layer + 1], axis=0, keepdims=True) if layer else jnp.zeros((1, HG_DK), F32)
    ng = ng_ref[...]
    st_ref[...] = jnp.zeros_like(st_ref)
    row = lax.broadcasted_iota(jnp.int32, (c, LANES), 0)
    rin = row & (SUBLANES - 1)

    def chunk(n, carry):
        r0 = pl.multiple_of(n * c, c)
        q = q_ref[pl.ds(r0, c), :]
        f = lb + (1.0 - lb) * _sigmoid(f_ref[pl.ds(r0, c), :])
        kk = 1.0 - f
        v = v_ref[pl.ds(r0, c), :]
        vb = v.astype(BF16)
        cum = _cumsum_rows(jnp.log2(f), rin)
        last = _bcast_row(cum, c - 1, c)

        st = st_ref[...]
        qi = (q * jnp.exp2(cum)).astype(BF16)
        o = lax.dot_general(qi, st.astype(BF16), _NT, preferred_element_type=F32)

        ks = (kk * jnp.exp2(-jnp.abs(last - cum))).astype(BF16)
        st_ref[...] = (st * jnp.exp2(last[0:1, :])
                       + lax.dot_general(vb, ks, _TN, preferred_element_type=F32))

        scores = jnp.zeros((c, c), F32)
        w = c // 2
        for lvl in range(n_levels):
            second = (row & w) != 0
            if w > 1:
                x = jnp.where(second, q, kk) * jnp.exp2(-jnp.abs(cum - _group_mid(cum, w)))
            else:
                x = jnp.where(second, q * f, kk)
            x = x.astype(BF16)
            a = lax.dot_general(x, x, _NT, preferred_element_type=F32)
            scores = scores + a * masks_ref[lvl]
            w //= 2
        o = o + jnp.dot(scores.astype(BF16), vb, preferred_element_type=F32)
        o = o + jnp.sum(q * kk, axis=-1, keepdims=True) * v

        ms = jnp.mean(o * o, axis=-1, keepdims=True)
        y = o * lax.rsqrt(ms + EPS) * ng * _silu(gate_ref[pl.ds(r0, c), :])
        o_ref[pl.ds(r0, c), :] = y.astype(o_ref.dtype)
        return carry

    lax.fori_loop(0, n_chunks, chunk, 0, unroll=HG_UNROLL)


def _hgrn(proj, lb_param, ng, layer, batch, seq):
    masks = _hgrn_masks()
    t = proj.shape[0]
    nh = HG_HEADS
    depth = lb_param.shape[0]

    def col(base):
        return pl.BlockSpec((seq, LANES), lambda b, h: (b, base + h))

    def const(shape):
        return pl.BlockSpec(shape, lambda b, h: (0,) * len(shape))

    return pl.pallas_call(
        functools.partial(_hgrn_kernel, layer=layer),
        grid=(batch, nh),
        in_specs=[
            col(0), col(nh), col(2 * nh), col(3 * nh),
            pl.BlockSpec((depth, LANES), lambda b, h: (0, h)),
            pl.BlockSpec((1, LANES), lambda b, h: (0, h)),
            const(masks.shape),
        ],
        out_specs=pl.BlockSpec((seq, LANES), lambda b, h: (b, h)),
        out_shape=jax.ShapeDtypeStruct((t, nh * HG_DV), BF16),
        scratch_shapes=[pltpu.VMEM((HG_DV, HG_DK), F32)],
        compiler_params=pltpu.CompilerParams(
            dimension_semantics=("parallel", "parallel"),
            vmem_limit_bytes=VMEM_LIMIT_BYTES),
        name="hgrn2",
    )(proj, proj, proj, proj, lb_param, ng, jnp.asarray(masks))


def _lru_kernel(x_ref, gate_ref, cw_ref, cb_ref, wri_ref, bri_ref, lam_ref, o_ref):
    rws = LRU_ROWS
    n_steps = x_ref.shape[0] // rws
    nv = rws // SUBLANES
    cw = cw_ref[...]
    cb = cb_ref[...]
    bri = bri_ref[...]
    sp = jnp.log1p(jnp.exp(-jnp.abs(lam_ref[...]))) + jnp.maximum(-lam_ref[...], 0.0)
    row8 = lax.broadcasted_iota(jnp.int32, (SUBLANES, LANES), 0)
    rin = lax.broadcasted_iota(jnp.int32, (rws, LANES), 0) & (SUBLANES - 1)

    def step(n, carry):
        prev8, hlast = carry
        r0 = pl.multiple_of(n * rws, rws)
        x = x_ref[pl.ds(r0, rws), :]

        xc = x * cw[CONV_WIDTH - 1:CONV_WIDTH, :] + cb
        for s in range(1, CONV_WIDTH):
            rolled = pltpu.roll(x, s, 0)
            top = jnp.where(row8 < s, pltpu.roll(prev8, s, 0), rolled[:SUBLANES])
            xs = jnp.concatenate([top, rolled[SUBLANES:]], axis=0)
            xc = xc + xs * cw[CONV_WIDTH - 1 - s:CONV_WIDTH - s, :]

        ri = jnp.dot(xc.astype(BF16), wri_ref[0], preferred_element_type=F32) + bri
        r = _sigmoid(ri[:, :LANES])
        i = _sigmoid(ri[:, LANES:])
        log_a = (-LRU_C) * r * sp
        a = jnp.exp(log_a)
        u = jnp.sqrt(-jnp.tanh(log_a) * (a * a + 1.0)) * (i * xc)

        for k in (1, 2, 4):
            valid = rin >= k
            u = u + a * jnp.where(valid, pltpu.roll(u, k, 0), 0.0)
            a = a * jnp.where(valid, pltpu.roll(a, k, 0), 1.0)
        hs = []
        for i8 in range(nv):
            sl = slice(i8 * SUBLANES, (i8 + 1) * SUBLANES)
            h8 = a[sl] * hlast + u[sl]
            hlast = jnp.broadcast_to(h8[SUBLANES - 1:SUBLANES, :], (SUBLANES, LANES))
            hs.append(h8)
        h = jnp.concatenate(hs, axis=0)

        y = h * _silu(gate_ref[pl.ds(r0, rws), :])
        o_ref[pl.ds(r0, rws), :] = y.astype(o_ref.dtype)
        return x[rws - SUBLANES:, :], hlast

    zeros8 = jnp.zeros((SUBLANES, LANES), F32)
    lax.fori_loop(0, n_steps, step, (zeros8, zeros8))


def _lru(proj, cw, cb, wri, bri, lam, batch, seq, x_col, gate_col):
    t = proj.shape[0]
    nb = LRU_BLOCKS

    def col(base):
        return pl.BlockSpec((seq, LANES), lambda b, n: (b, base + n))

    def vec(rows):
        return pl.BlockSpec((rows, LANES), lambda b, n: (0, n))

    return pl.pallas_call(
        _lru_kernel,
        grid=(batch, nb),
        in_specs=[
            col(x_col), col(gate_col),
            vec(CONV_WIDTH), vec(1),
            pl.BlockSpec((1, LRU_BLOCK, 2 * LRU_BLOCK), lambda b, n: (n, 0, 0)),
            pl.BlockSpec((1, 2 * LANES), lambda b, n: (0, n)),
            vec(1),
        ],
        out_specs=pl.BlockSpec((seq, LANES), lambda b, n: (b, n)),
        out_shape=jax.ShapeDtypeStruct((t, nb * LRU_BLOCK), BF16),
        compiler_params=pltpu.CompilerParams(
            dimension_semantics=("parallel", "parallel"),
            vmem_limit_bytes=VMEM_LIMIT_BYTES),
        name="rglru",
    )(proj, proj, cw, cb, wri, bri, lam)


def _kv_kernel(mem_ref, g_ref, w_ref, o_ref, h_ref):
    @pl.when(pl.program_id(1) == 0)
    def _():
        x = mem_ref[...]
        ms = jnp.mean(x * x, axis=-1, keepdims=True)
        h_ref[...] = (x * lax.rsqrt(ms + EPS) * g_ref[0]).astype(BF16)

    o_ref[0] = jnp.dot(h_ref[...], w_ref[0], preferred_element_type=F32).astype(o_ref.dtype)


def _kv(mem2, g, w_bf16, tn=512):
    depth, d, n = w_bf16.shape
    m = mem2.shape[0]
    return pl.pallas_call(
        _kv_kernel,
        grid=(depth, n // tn),
        in_specs=[
            pl.BlockSpec((m, d), lambda l, j: (0, 0)),
            pl.BlockSpec((1, 1, d), lambda l, j: (l, 0, 0)),
            pl.BlockSpec((1, d, tn), lambda l, j: (l, 0, j)),
        ],
        out_specs=pl.BlockSpec((1, m, tn), lambda l, j: (l, 0, j)),
        out_shape=jax.ShapeDtypeStruct((depth, m, n), BF16),
        scratch_shapes=[pltpu.VMEM((m, d), BF16)],
        compiler_params=pltpu.CompilerParams(
            dimension_semantics=("parallel", "arbitrary"),
            vmem_limit_bytes=VMEM_LIMIT_BYTES),
        name="mem_kv",
    )(mem2, g, w_bf16)


def _attn_kernel(q_ref, gate_ref, k_ref, v_ref, o_ref):
    scale = XA_HEAD_DIM ** -0.5
    for h in range(XA_HEADS):
        sl = slice(h * XA_HEAD_DIM, (h + 1) * XA_HEAD_DIM)
        s = lax.dot_general(q_ref[:, sl].astype(BF16), k_ref[0, :, sl], _NT,
                            preferred_element_type=F32) * scale
        m = jnp.max(s, axis=-1, keepdims=True)
        p = jnp.exp(s - m)
        p = p / jnp.sum(p, axis=-1, keepdims=True)
        o = jnp.dot(p.astype(BF16), v_ref[0, :, sl], preferred_element_type=F32)
        o_ref[:, sl] = (o * _silu(gate_ref[:, sl])).astype(o_ref.dtype)


def _attn(proj, kv, layer, batch, seq, n_mem, q_col, gate_col, ts=512):
    t = proj.shape[0]
    width = XA_HEADS * XA_HEAD_DIM
    nt = seq // ts
    return pl.pallas_call(
        _attn_kernel,
        grid=(batch, nt),
        in_specs=[
            pl.BlockSpec((ts, width), lambda b, i: (b * nt + i, q_col)),
            pl.BlockSpec((ts, width), lambda b, i: (b * nt + i, gate_col)),
            pl.BlockSpec((1, n_mem, width), lambda b, i: (layer * batch + b, 0, 0)),
            pl.BlockSpec((1, n_mem, width), lambda b, i: (layer * batch + b, 0, 1)),
        ],
        out_specs=pl.BlockSpec((ts, width), lambda b, i: (b * nt + i, 0)),
        out_shape=jax.ShapeDtypeStruct((t, width), BF16),
        compiler_params=pltpu.CompilerParams(
            dimension_semantics=("parallel", "parallel"),
            vmem_limit_bytes=VMEM_LIMIT_BYTES),
        name="mem_attn",
    )(proj, proj, kv, kv)


def _outproj_kernel(x_ref, a_ref, b_ref, c_ref, w_ref, o_ref):
    ka = a_ref.shape[1]
    kb = b_ref.shape[1]
    acc = x_ref[...]
    acc = acc + jnp.dot(a_ref[...], w_ref[0:ka, :], preferred_element_type=F32)
    acc = acc + jnp.dot(b_ref[...], w_ref[ka:ka + kb, :], preferred_element_type=F32)
    acc = acc + jnp.dot(c_ref[...], w_ref[ka + kb:, :], preferred_element_type=F32)
    o_ref[...] = acc


def _outproj(x2, oa, ob, oc, w_bf16, tm=1024, tn=512):
    t, d = x2.shape
    k = w_bf16.shape[0]

    def lhs(arr):
        return pl.BlockSpec((tm, arr.shape[1]), lambda i, j: (i, 0))

    return pl.pallas_call(
        _outproj_kernel,
        grid=(t // tm, d // tn),
        in_specs=[
            pl.BlockSpec((tm, tn), lambda i, j: (i, j)),
            lhs(oa), lhs(ob), lhs(oc),
            pl.BlockSpec((k, tn), lambda i, j: (0, j)),
        ],
        out_specs=pl.BlockSpec((tm, tn), lambda i, j: (i, j)),
        out_shape=jax.ShapeDtypeStruct((t, d), F32),
        compiler_params=pltpu.CompilerParams(
            dimension_semantics=("parallel", "arbitrary"),
            vmem_limit_bytes=VMEM_LIMIT_BYTES),
        name="outproj",
    )(x2, oa, ob, oc, w_bf16)


def _norm_kernel(x_ref, g_ref, o_ref):
    x = x_ref[...]
    ms = jnp.mean(x * x, axis=-1, keepdims=True)
    o_ref[...] = x * lax.rsqrt(ms + EPS) * g_ref[...]


def _final_norm(x2, g, tm=512):
    t, d = x2.shape
    return pl.pallas_call(
        _norm_kernel,
        grid=(t // tm,),
        in_specs=[pl.BlockSpec((tm, d), lambda i: (i, 0)),
                  pl.BlockSpec((1, d), lambda i: (0, 0))],
        out_specs=pl.BlockSpec((tm, d), lambda i: (i, 0)),
        out_shape=jax.ShapeDtypeStruct((t, d), F32),
        compiler_params=pltpu.CompilerParams(
            dimension_semantics=("parallel",),
            vmem_limit_bytes=VMEM_LIMIT_BYTES),
        name="final_norm",
    )(x2, g)


def kernel(x, mem, norm_g, w_in, lb_param, hg_norm_g, conv_w, conv_b, w_r, b_r, w_i, b_i, lam,
           mem_norm_g, w_kv, w_out, final_g):
    batch, seq, d_model = x.shape
    depth = w_in.shape[0]
    n_mem = mem.shape[1]
    hg_key = HG_HEADS * HG_DK
    hg_val = HG_HEADS * HG_DV
    lru_w = LRU_BLOCKS * LRU_BLOCK
    xa_w = XA_HEADS * XA_HEAD_DIM
    assert seq % (HG_CHUNK * HG_UNROLL) == 0 and seq % LRU_ROWS == 0
    assert w_in.shape[2] == 2 * hg_key + 2 * hg_val + 2 * lru_w + 2 * xa_w
    assert hg_key == hg_val == lru_w == xa_w
    lru_x_col = (2 * hg_key + 2 * hg_val) // LANES
    lru_gate_col = lru_x_col + lru_w // LANES
    xa_q_col = (2 * hg_key + 2 * hg_val + 2 * lru_w) // xa_w
    xa_gate_col = xa_q_col + 1

    w_in_b = w_in.astype(BF16)
    w_out_b = w_out.astype(BF16)
    w_kv_b = w_kv.astype(BF16)
    wri = jnp.concatenate([w_r, w_i], axis=-1).astype(BF16)
    bri = jnp.concatenate([b_r, b_i], axis=-1).reshape(depth, 1, -1)

    x2 = x.reshape(batch * seq, d_model)
    kv = _kv(mem.reshape(batch * n_mem, d_model), mem_norm_g.reshape(depth, 1, d_model), w_kv_b)
    kv = kv.reshape(depth * batch, n_mem, 2 * xa_w)

    for l in range(depth):
        proj = _inproj(x2, norm_g[l].reshape(1, d_model), w_in_b[l])
        o_a = _hgrn(proj, lb_param, hg_norm_g[l].reshape(1, hg_val), l, batch, seq)
        o_b = _lru(proj, conv_w[l], conv_b[l].reshape(1, lru_w), wri[l], bri[l],
                   lam[l].reshape(1, lru_w), batch, seq, lru_x_col, lru_gate_col)
        o_c = _attn(proj, kv, l, batch, seq, n_mem, xa_q_col, xa_gate_col)
        x2 = _outproj(x2, o_a, o_b, o_c, w_out_b[l])

    return _final_norm(x2, final_g.reshape(1, d_model)).reshape(batch, seq, d_model)
```

```python
import functools

import numpy as np
import jax
import jax.numpy as jnp
from jax import lax
from jax.experimental import pallas as pl
from jax.experimental.pallas import tpu as pltpu

F32 = jnp.float32
BF16 = jnp.bfloat16

EPS = 1e-6
LRU_C = 8.0
CONV_WIDTH = 4

HG_HEADS = 8
HG_DK = 128
HG_DV = 128
LRU_BLOCKS = 8
LRU_BLOCK = 128
XA_HEADS = 4
XA_HEAD_DIM = 256

SUBLANES = 8
LANES = 128
VMEM_LIMIT_BYTES = 56 * 1024 * 1024

HG_CHUNK = 64
HG_UNROLL = 8
LRU_ROWS = 256


def _half_tanh(x):
    h = 0.5 * x
    return h, jnp.tanh(h)


def _sigmoid(x):
    _, th = _half_tanh(x)
    return 0.5 + 0.5 * th


def _silu(x):
    h, th = _half_tanh(x)
    return h + h * th


def _tile(x, i):
    return x[i * SUBLANES:(i + 1) * SUBLANES]


def _bcast_row(tile, r):
    return jnp.broadcast_to(tile[r:r + 1, :], tile.shape)


def _inproj_kernel(x_ref, g_ref, w_ref, o_ref, h_ref):
    @pl.when(pl.program_id(1) == 0)
    def _():
        x = x_ref[...]
        ms = jnp.mean(x * x, axis=-1, keepdims=True)
        h_ref[...] = (x * lax.rsqrt(ms + EPS) * g_ref[...]).astype(BF16)

    o_ref[...] = jnp.dot(h_ref[...], w_ref[...], preferred_element_type=F32)


def _inproj(x2, g, w_bf16, layer, tm=1024, tn=1024):
    t, d = x2.shape
    n = w_bf16.shape[2]
    return pl.pallas_call(
        _inproj_kernel,
        grid=(t // tm, n // tn),
        in_specs=[
            pl.BlockSpec((tm, d), lambda i, j: (i, 0)),
            pl.BlockSpec((None, 1, d), lambda i, j: (layer, 0, 0)),
            pl.BlockSpec((None, d, tn), lambda i, j: (layer, 0, j)),
        ],
        out_specs=pl.BlockSpec((tm, tn), lambda i, j: (i, j)),
        out_shape=jax.ShapeDtypeStruct((t, n), F32),
        scratch_shapes=[pltpu.VMEM((tm, d), BF16)],
        compiler_params=pltpu.CompilerParams(
            dimension_semantics=("parallel", "arbitrary"),
            vmem_limit_bytes=VMEM_LIMIT_BYTES),
        name="inproj",
    )(x2, g, w_bf16)


def _hgrn_level_widths():
    w = HG_CHUNK // 2
    widths = []
    while w >= 1:
        widths.append(w)
        w //= 2
    return widths


def _hgrn_masks():
    t = np.arange(HG_CHUNK)[:, None]
    s = np.arange(HG_CHUNK)[None, :]
    masks = [((t // (2 * w)) == (s // (2 * w))) & ((t % (2 * w)) >= w) & ((s % (2 * w)) < w)
             for w in _hgrn_level_widths()]
    return np.stack(masks).astype(np.float32)


_NT = (((1,), (1,)), ((), ()))
_TN = (((0,), (0,)), ((), ()))


def _hgrn_kernel(q_ref, f_ref, v_ref, gate_ref, lbp_ref, ng_ref, masks_ref, o_ref,
                 st_ref, xs_ref, qi_ref, ks_ref, vb_ref, dec_ref, od_ref, sc_ref, op_ref, vb2_ref,
                 *, layer):
    c = HG_CHUNK
    nt = c // SUBLANES
    widths = _hgrn_level_widths()
    n_chunks = q_ref.shape[0] // c
    lbp = lbp_ref[...]
    lbe = jnp.exp(lbp - jnp.max(lbp, axis=0, keepdims=True))
    sm = lbe / jnp.sum(lbe, axis=0, keepdims=True)
    lb = jnp.sum(sm[1:layer + 1], axis=0, keepdims=True) if layer else jnp.zeros((1, HG_DK), F32)
    f_mid = 0.5 * (1.0 + lb)
    f_half = 0.5 * (1.0 - lb)
    ng = ng_ref[...]

    row8 = lax.broadcasted_iota(jnp.int32, (SUBLANES, LANES), 0)
    scan_masks = [(k, (row8 >= k).astype(F32)) for k in (1, 2, 4)]
    second = {w: (row8 & w) != 0 for w in (4, 2, 1)}
    sign = {w: jnp.where(second[w], 1.0, -1.0) for w in (4, 2)}
    low4 = row8 < 4

    def rows(idx):
        return pl.ds(pl.multiple_of(idx * c, c), c)

    def front(idx):
        sl = rows(idx)
        q = q_ref[sl, :]
        v = v_ref[sl, :]
        f = f_mid + f_half * jnp.tanh(0.5 * f_ref[sl, :])
        kk = 1.0 - f
        g = jnp.log2(f)

        cum, ends = [], []
        for i in range(nt):
            t = _tile(g, i)
            for k, m in scan_masks:
                t = t + pltpu.roll(t, k, 0) * m
            if i:
                t = t + ends[-1]
            cum.append(t)
            ends.append(_bcast_row(t, SUBLANES - 1))

        xs = []
        for w in widths:
            tiles = []
            for i in range(nt):
                qt, kt, ct = _tile(q, i), _tile(kk, i), cum[i]
                if w >= SUBLANES:
                    r = i * SUBLANES
                    mid = ends[(r - r % (2 * w) + w) // SUBLANES - 1]
                    x = qt * jnp.exp2(ct - mid) if r & w else kt * jnp.exp2(mid - ct)
                elif w == 4:
                    x = jnp.where(second[4], qt, kt) * jnp.exp2((ct - _bcast_row(ct, 3)) * sign[4])
                elif w == 2:
                    mid = jnp.where(low4, _bcast_row(ct, 1), _bcast_row(ct, 5))
                    x = jnp.where(second[2], qt, kt) * jnp.exp2((ct - mid) * sign[2])
                else:
                    x = jnp.where(second[1], qt * _tile(f, i), kt)
                tiles.append(x)
            xs.append(jnp.concatenate(tiles, axis=0).astype(BF16))

        last = ends[-1]
        qi = jnp.concatenate([_tile(q, i) * jnp.exp2(cum[i]) for i in range(nt)], axis=0)
        ks = jnp.concatenate([_tile(kk, i) * jnp.exp2(last - cum[i]) for i in range(nt)], axis=0)
        dec = jnp.exp2(last)
        od = jnp.sum(q * kk, axis=-1, keepdims=True) * v
        return xs, qi.astype(BF16), ks.astype(BF16), v.astype(BF16), dec, od

    def put_front(xs, qi, ks, vb, dec, od):
        for lvl in range(len(widths)):
            xs_ref[lvl] = xs[lvl]
        qi_ref[...] = qi
        ks_ref[...] = ks
        vb_ref[...] = vb
        dec_ref[...] = dec
        od_ref[...] = od

    def mid_start():
        a = [lax.dot_general(xs_ref[lvl], xs_ref[lvl], _NT, preferred_element_type=F32)
             for lvl in range(len(widths))]
        vb = vb_ref[...]
        upd = lax.dot_general(vb, ks_ref[...], _TN, preferred_element_type=F32)
        st = st_ref[...]
        o_inter = lax.dot_general(qi_ref[...], st.astype(BF16), _NT, preferred_element_type=F32)
        return st, vb, o_inter, upd, a

    def mid_finish(st, vb, o_inter, upd, a):
        tiles = []
        for i in range(nt):
            r = i * SUBLANES
            acc = None
            for lvl, w in enumerate(widths):
                if w >= SUBLANES and not r & w:
                    continue
                term = _tile(a[lvl], i) * masks_ref[lvl, r:r + SUBLANES, :]
                acc = term if acc is None else acc + term
            tiles.append(acc)
        sc_ref[...] = jnp.concatenate(tiles, axis=0).astype(BF16)
        op_ref[...] = o_inter + od_ref[...]
        vb2_ref[...] = vb
        st_ref[...] = st * dec_ref[0:1, :] + upd

    def back_start():
        return op_ref[...] + jnp.dot(sc_ref[...], vb2_ref[...], preferred_element_type=F32)

    def back_finish(o, idx):
        ms = jnp.mean(o * o, axis=-1, keepdims=True)
        y = o * lax.rsqrt(ms + EPS) * ng * _silu(gate_ref[rows(idx), :])
        o_ref[rows(idx), :] = y.astype(o_ref.dtype)

    st_ref[...] = jnp.zeros_like(st_ref)
    sc_ref[...] = jnp.zeros_like(sc_ref)
    op_ref[...] = jnp.zeros_like(op_ref)
    vb2_ref[...] = jnp.zeros_like(vb2_ref)
    put_front(*front(0))

    def body(i, carry):
        mid = mid_start()
        o_back = back_start()
        nxt = front(jnp.minimum(i + 1, n_chunks - 1))
        back_finish(o_back, jnp.maximum(i - 1, 0))
        mid_finish(*mid)
        put_front(*nxt)
        return carry

    lax.fori_loop(0, n_chunks, body, 0, unroll=HG_UNROLL)
    back_finish(back_start(), n_chunks - 1)


def _hgrn(proj, lb_param, ng, layer, batch, seq):
    masks = _hgrn_masks()
    c = HG_CHUNK
    t = proj.shape[0]
    nh = HG_HEADS
    depth = lb_param.shape[0]

    def col(base):
        return pl.BlockSpec((seq, LANES), lambda b, h: (b, base + h))

    return pl.pallas_call(
        functools.partial(_hgrn_kernel, layer=layer),
        grid=(batch, nh),
        in_specs=[
            col(0), col(nh), col(2 * nh), col(3 * nh),
            pl.BlockSpec((depth, LANES), lambda b, h: (0, h)),
            pl.BlockSpec((None, 1, LANES), lambda b, h: (layer, 0, h)),
            pl.BlockSpec(masks.shape, lambda b, h: (0, 0, 0)),
        ],
        out_specs=pl.BlockSpec((seq, LANES), lambda b, h: (b, h)),
        out_shape=jax.ShapeDtypeStruct((t, nh * HG_DV), BF16),
        scratch_shapes=[
            pltpu.VMEM((HG_DV, HG_DK), F32),
            pltpu.VMEM((masks.shape[0], c, HG_DK), BF16),
            pltpu.VMEM((c, HG_DK), BF16),
            pltpu.VMEM((c, HG_DK), BF16),
            pltpu.VMEM((c, HG_DV), BF16),
            pltpu.VMEM((SUBLANES, HG_DK), F32),
            pltpu.VMEM((c, HG_DV), F32),
            pltpu.VMEM((c, c), BF16),
            pltpu.VMEM((c, HG_DV), F32),
            pltpu.VMEM((c, HG_DV), BF16),
        ],
        compiler_params=pltpu.CompilerParams(
            dimension_semantics=("parallel", "parallel"),
            vmem_limit_bytes=VMEM_LIMIT_BYTES),
        name="hgrn2",
    )(proj, proj, proj, proj, lb_param, ng, jnp.asarray(masks))


def _lru_kernel(x_ref, gate_ref, cw_ref, cb_ref, wri_ref, bri_ref, lam_ref, o_ref):
    rws = LRU_ROWS
    nt = rws // SUBLANES
    n_steps = x_ref.shape[0] // rws
    cw = cw_ref[...]
    cb = cb_ref[...]
    bri = bri_ref[...]
    lam = lam_ref[...]
    log_a_scale = (-LRU_C) * (jnp.log1p(jnp.exp(-jnp.abs(lam))) + jnp.maximum(-lam, 0.0))
    tap = [cw[CONV_WIDTH - 1 - s:CONV_WIDTH - s, :] for s in range(CONV_WIDTH)]
    row8 = lax.broadcasted_iota(jnp.int32, (SUBLANES, LANES), 0)
    from_prev = [row8 < s for s in range(CONV_WIDTH)]
    scan_masks = [(k, row8 >= k) for k in (1, 2, 4)]

    def step(n, carry):
        prev8, hlast = carry
        r0 = pl.multiple_of(n * rws, rws)
        x = x_ref[pl.ds(r0, rws), :]

        prev_rolled = [pltpu.roll(prev8, s, 0) for s in range(1, CONV_WIDTH)]
        xc_tiles = []
        for j in range(nt):
            t = _tile(x, j)
            rolled = [pltpu.roll(t, s, 0) for s in range(1, CONV_WIDTH)]
            acc = t * tap[0] + cb
            for s in range(1, CONV_WIDTH):
                acc = acc + jnp.where(from_prev[s], prev_rolled[s - 1], rolled[s - 1]) * tap[s]
            prev_rolled = rolled
            xc_tiles.append(acc)
        xc = jnp.concatenate(xc_tiles, axis=0)

        ri = jnp.dot(xc.astype(BF16), wri_ref[...], preferred_element_type=F32) + bri
        log_a = _sigmoid(ri[:, :LANES]) * log_a_scale
        a = jnp.exp(log_a)
        u = jnp.sqrt(-jnp.tanh(log_a) * (a * a + 1.0)) * (_sigmoid(ri[:, LANES:]) * xc)

        hs = []
        for j in range(nt):
            at, ut = _tile(a, j), _tile(u, j)
            for k, m in scan_masks:
                ut = ut + at * jnp.where(m, pltpu.roll(ut, k, 0), 0.0)
                at = at * jnp.where(m, pltpu.roll(at, k, 0), 1.0)
            h8 = at * hlast + ut
            hlast = _bcast_row(h8, SUBLANES - 1)
            hs.append(h8)
        h = jnp.concatenate(hs, axis=0)

        y = h * _silu(gate_ref[pl.ds(r0, rws), :])
        o_ref[pl.ds(r0, rws), :] = y.astype(o_ref.dtype)
        return _tile(x, nt - 1), hlast

    zeros8 = jnp.zeros((SUBLANES, LANES), F32)
    lax.fori_loop(0, n_steps, step, (zeros8, zeros8))


def _lru(proj, cw, cb, wri, bri, lam, layer, batch, seq, x_col, gate_col):
    t = proj.shape[0]
    nb = LRU_BLOCKS

    def col(base):
        return pl.BlockSpec((seq, LANES), lambda b, n: (b, base + n))

    def vec(rows, width=LANES):
        return pl.BlockSpec((None, rows, width), lambda b, n: (layer, 0, n))

    return pl.pallas_call(
        _lru_kernel,
        grid=(batch, nb),
        in_specs=[
            col(x_col), col(gate_col),
            vec(CONV_WIDTH), vec(1),
            pl.BlockSpec((None, None, LRU_BLOCK, 2 * LRU_BLOCK), lambda b, n: (layer, n, 0, 0)),
            vec(1, 2 * LANES),
            vec(1),
        ],
        out_specs=pl.BlockSpec((seq, LANES), lambda b, n: (b, n)),
        out_shape=jax.ShapeDtypeStruct((t, nb * LRU_BLOCK), BF16),
        compiler_params=pltpu.CompilerParams(
            dimension_semantics=("parallel", "parallel"),
            vmem_limit_bytes=VMEM_LIMIT_BYTES),
        name="rglru",
    )(proj, proj, cw, cb, wri, bri, lam)


def _kv_kernel(mem_ref, g_ref, w_ref, o_ref, h_ref):
    @pl.when(pl.program_id(1) == 0)
    def _():
        x = mem_ref[...]
        ms = jnp.mean(x * x, axis=-1, keepdims=True)
        h_ref[...] = (x * lax.rsqrt(ms + EPS) * g_ref[...]).astype(BF16)

    o_ref[...] = jnp.dot(h_ref[...], w_ref[...], preferred_element_type=F32).astype(o_ref.dtype)


def _kv(mem2, g, w_bf16, tn=512):
    depth, d, n = w_bf16.shape
    m = mem2.shape[0]
    return pl.pallas_call(
        _kv_kernel,
        grid=(depth, n // tn),
        in_specs=[
            pl.BlockSpec((m, d), lambda l, j: (0, 0)),
            pl.BlockSpec((None, 1, d), lambda l, j: (l, 0, 0)),
            pl.BlockSpec((None, d, tn), lambda l, j: (l, 0, j)),
        ],
        out_specs=pl.BlockSpec((None, m, tn), lambda l, j: (l, 0, j)),
        out_shape=jax.ShapeDtypeStruct((depth, m, n), BF16),
        scratch_shapes=[pltpu.VMEM((m, d), BF16)],
        compiler_params=pltpu.CompilerParams(
            dimension_semantics=("parallel", "arbitrary"),
            vmem_limit_bytes=VMEM_LIMIT_BYTES),
        name="mem_kv",
    )(mem2, g, w_bf16)


def _attn_kernel(q_ref, gate_ref, k_ref, v_ref, o_ref):
    scale = XA_HEAD_DIM ** -0.5
    for h in range(XA_HEADS):
        sl = slice(h * XA_HEAD_DIM, (h + 1) * XA_HEAD_DIM)
        s = lax.dot_general(q_ref[:, sl].astype(BF16), k_ref[:, sl], _NT,
                            preferred_element_type=F32) * scale
        m = jnp.max(s, axis=-1, keepdims=True)
        p = jnp.exp(s - m)
        p = p / jnp.sum(p, axis=-1, keepdims=True)
        o = jnp.dot(p.astype(BF16), v_ref[:, sl], preferred_element_type=F32)
        o_ref[:, sl] = (o * _silu(gate_ref[:, sl])).astype(o_ref.dtype)


def _attn(proj, kv, layer, batch, seq, n_mem, q_col, gate_col, ts=512):
    t = proj.shape[0]
    width = XA_HEADS * XA_HEAD_DIM
    nt = seq // ts
    return pl.pallas_call(
        _attn_kernel,
        grid=(batch, nt),
        in_specs=[
            pl.BlockSpec((ts, width), lambda b, i: (b * nt + i, q_col)),
            pl.BlockSpec((ts, width), lambda b, i: (b * nt + i, gate_col)),
            pl.BlockSpec((None, n_mem, width), lambda b, i: (layer * batch + b, 0, 0)),
            pl.BlockSpec((None, n_mem, width), lambda b, i: (layer * batch + b, 0, 1)),
        ],
        out_specs=pl.BlockSpec((ts, width), lambda b, i: (b * nt + i, 0)),
        out_shape=jax.ShapeDtypeStruct((t, width), BF16),
        compiler_params=pltpu.CompilerParams(
            dimension_semantics=("parallel", "parallel"),
            vmem_limit_bytes=VMEM_LIMIT_BYTES),
        name="mem_attn",
    )(proj, proj, kv, kv)


def _outproj_kernel(x_ref, a_ref, b_ref, c_ref, w_ref, o_ref):
    ka = a_ref.shape[1]
    kb = b_ref.shape[1]
    acc = x_ref[...]
    acc = acc + jnp.dot(a_ref[...], w_ref[0:ka, :], preferred_element_type=F32)
    acc = acc + jnp.dot(b_ref[...], w_ref[ka:ka + kb, :], preferred_element_type=F32)
    acc = acc + jnp.dot(c_ref[...], w_ref[ka + kb:, :], preferred_element_type=F32)
    o_ref[...] = acc


def _outproj(x2, oa, ob, oc, w_bf16, layer, tm=1024, tn=1024):
    t, d = x2.shape
    k = w_bf16.shape[1]

    def lhs(arr):
        return pl.BlockSpec((tm, arr.shape[1]), lambda i, j: (i, 0))

    return pl.pallas_call(
        _outproj_kernel,
        grid=(t // tm, d // tn),
        in_specs=[
            pl.BlockSpec((tm, tn), lambda i, j: (i, j)),
            lhs(oa), lhs(ob), lhs(oc),
            pl.BlockSpec((None, k, tn), lambda i, j: (layer, 0, j)),
        ],
        out_specs=pl.BlockSpec((tm, tn), lambda i, j: (i, j)),
        out_shape=jax.ShapeDtypeStruct((t, d), F32),
        compiler_params=pltpu.CompilerParams(
            dimension_semantics=("parallel", "arbitrary"),
            vmem_limit_bytes=VMEM_LIMIT_BYTES),
        name="outproj",
    )(x2, oa, ob, oc, w_bf16)


def _norm_kernel(x_ref, g_ref, o_ref):
    x = x_ref[...]
    ms = jnp.mean(x * x, axis=-1, keepdims=True)
    o_ref[...] = x * lax.rsqrt(ms + EPS) * g_ref[...]


def _final_norm(x2, g, tm=512):
    t, d = x2.shape
    return pl.pallas_call(
        _norm_kernel,
        grid=(t // tm,),
        in_specs=[pl.BlockSpec((tm, d), lambda i: (i, 0)),
                  pl.BlockSpec((1, d), lambda i: (0, 0))],
        out_specs=pl.BlockSpec((tm, d), lambda i: (i, 0)),
        out_shape=jax.ShapeDtypeStruct((t, d), F32),
        compiler_params=pltpu.CompilerParams(
            dimension_semantics=("parallel",),
            vmem_limit_bytes=VMEM_LIMIT_BYTES),
        name="final_norm",
    )(x2, g)


def kernel(x, mem, norm_g, w_in, lb_param, hg_norm_g, conv_w, conv_b, w_r, b_r, w_i, b_i, lam,
           mem_norm_g, w_kv, w_out, final_g):
    batch, seq, d_model = x.shape
    depth = w_in.shape[0]
    n_mem = mem.shape[1]
    hg_key = HG_HEADS * HG_DK
    hg_val = HG_HEADS * HG_DV
    lru_w = LRU_BLOCKS * LRU_BLOCK
    xa_w = XA_HEADS * XA_HEAD_DIM
    assert seq % (HG_CHUNK * HG_UNROLL) == 0 and seq % LRU_ROWS == 0
    assert w_in.shape[2] == 2 * hg_key + 2 * hg_val + 2 * lru_w + 2 * xa_w
    assert hg_key == hg_val == lru_w == xa_w
    lru_x_col = (2 * hg_key + 2 * hg_val) // LANES
    lru_gate_col = lru_x_col + lru_w // LANES
    xa_q_col = (2 * hg_key + 2 * hg_val + 2 * lru_w) // xa_w
    xa_gate_col = xa_q_col + 1

    w_in_b = w_in.astype(BF16)
    w_out_b = w_out.astype(BF16)
    w_kv_b = w_kv.astype(BF16)
    wri = jnp.concatenate([w_r, w_i], axis=-1).astype(BF16)
    bri = jnp.concatenate([b_r, b_i], axis=-1).reshape(depth, 1, -1)
    norm_g3 = norm_g.reshape(depth, 1, d_model)
    hg_norm_g3 = hg_norm_g.reshape(depth, 1, hg_val)
    conv_b3 = conv_b.reshape(depth, 1, lru_w)
    lam3 = lam.reshape(depth, 1, lru_w)

    x2 = x.reshape(batch * seq, d_model)
    kv = _kv(mem.reshape(batch * n_mem, d_model), mem_norm_g.reshape(depth, 1, d_model), w_kv_b)
    kv = kv.reshape(depth * batch, n_mem, 2 * xa_w)

    for l in range(depth):
        proj = _inproj(x2, norm_g3, w_in_b, l)
        o_a = _hgrn(proj, lb_param, hg_norm_g3, l, batch, seq)
        o_b = _lru(proj, conv_w, conv_b3, wri, bri, lam3, l, batch, seq, lru_x_col, lru_gate_col)
        o_c = _attn(proj, kv, l, batch, seq, n_mem, xa_q_col, xa_gate_col)
        x2 = _outproj(x2, o_a, o_b, o_c, w_out_b, l)

    return _final_norm(x2, final_g.reshape(1, d_model)).reshape(batch, seq, d_model)
```

```python
import functools

import numpy as np
import jax
import jax.numpy as jnp
from jax import lax
from jax.experimental import pallas as pl
from jax.experimental.pallas import tpu as pltpu

F32 = jnp.float32
BF16 = jnp.bfloat16

EPS = 1e-6
LRU_C = 8.0
CONV_WIDTH = 4

HG_HEADS = 8
HG_DK = 128
HG_DV = 128
LRU_BLOCKS = 8
LRU_BLOCK = 128
XA_HEADS = 4
XA_HEAD_DIM = 256

SUBLANES = 8
LANES = 128
VMEM_LIMIT_BYTES = 56 * 1024 * 1024

HG_CHUNK = 64
HG_UNROLL = 8
LRU_ROWS = 256


def _half_tanh(x):
    h = 0.5 * x
    return h, jnp.tanh(h)


def _sigmoid(x):
    _, th = _half_tanh(x)
    return 0.5 + 0.5 * th


def _silu(x):
    h, th = _half_tanh(x)
    return h + h * th


def _tile(x, i):
    return x[i * SUBLANES:(i + 1) * SUBLANES]


def _bcast_row(tile, r):
    return jnp.broadcast_to(tile[r:r + 1, :], tile.shape)


def _rmsnorm_rows(x, g):
    ms = jnp.mean(x * x, axis=-1, keepdims=True)
    return x * lax.rsqrt(ms + EPS) * g


def _norm_kernel(x_ref, g_ref, o_ref):
    o_ref[...] = _rmsnorm_rows(x_ref[...], g_ref[...]).astype(o_ref.dtype)


def _norm(x2, g3, layer, dtype, tm=512):
    t, d = x2.shape
    return pl.pallas_call(
        _norm_kernel,
        grid=(t // tm,),
        in_specs=[pl.BlockSpec((tm, d), lambda i: (i, 0)),
                  pl.BlockSpec((None, 1, d), lambda i: (layer, 0, 0))],
        out_specs=pl.BlockSpec((tm, d), lambda i: (i, 0)),
        out_shape=jax.ShapeDtypeStruct((t, d), dtype),
        compiler_params=pltpu.CompilerParams(
            dimension_semantics=("parallel",),
            vmem_limit_bytes=VMEM_LIMIT_BYTES),
        name="norm",
    )(x2, g3)


def _inproj_kernel(h_ref, w_ref, o_ref):
    o_ref[...] = jnp.dot(h_ref[...], w_ref[...], preferred_element_type=F32)


def _inproj(h_src, h_block, w_bf16, layer, seq, tn=1024):
    d = h_src.shape[1]
    n = w_bf16.shape[2]
    return pl.pallas_call(
        _inproj_kernel,
        grid=(n // tn,),
        in_specs=[
            pl.BlockSpec((seq, d), lambda j: (h_block, 0)),
            pl.BlockSpec((None, d, tn), lambda j: (layer, 0, j)),
        ],
        out_specs=pl.BlockSpec((seq, tn), lambda j: (0, j)),
        out_shape=jax.ShapeDtypeStruct((seq, n), F32),
        compiler_params=pltpu.CompilerParams(
            dimension_semantics=("arbitrary",),
            vmem_limit_bytes=VMEM_LIMIT_BYTES),
        name="inproj",
    )(h_src, w_bf16)


def _hgrn_level_widths():
    w = HG_CHUNK // 2
    widths = []
    while w >= 1:
        widths.append(w)
        w //= 2
    return widths


def _hgrn_masks():
    t = np.arange(HG_CHUNK)[:, None]
    s = np.arange(HG_CHUNK)[None, :]
    masks = [((t // (2 * w)) == (s // (2 * w))) & ((t % (2 * w)) >= w) & ((s % (2 * w)) < w)
             for w in _hgrn_level_widths()]
    return np.stack(masks).astype(np.float32)


_NT = (((1,), (1,)), ((), ()))
_TN = (((0,), (0,)), ((), ()))


def _hgrn_stages(q_ref, f_ref, v_ref, gate_ref, lbp_ref, ng_ref, masks_ref, o_ref,
                 st_ref, xs_ref, qi_ref, ks_ref, vb_ref, dec_ref, od_ref, sc_ref, op_ref, vb2_ref,
                 layer):
    c = HG_CHUNK
    nt = c // SUBLANES
    widths = _hgrn_level_widths()
    n_chunks = q_ref.shape[0] // c
    lbp = lbp_ref[...]
    lbe = jnp.exp(lbp - jnp.max(lbp, axis=0, keepdims=True))
    sm = lbe / jnp.sum(lbe, axis=0, keepdims=True)
    lb = jnp.sum(sm[1:layer + 1], axis=0, keepdims=True) if layer else jnp.zeros((1, HG_DK), F32)
    f_mid = 0.5 * (1.0 + lb)
    f_half = 0.5 * (1.0 - lb)
    ng = ng_ref[...]

    row8 = lax.broadcasted_iota(jnp.int32, (SUBLANES, LANES), 0)
    scan_masks = [(k, (row8 >= k).astype(F32)) for k in (1, 2, 4)]
    second = {w: (row8 & w) != 0 for w in (4, 2, 1)}
    sign = {w: jnp.where(second[w], 1.0, -1.0) for w in (4, 2)}
    low4 = row8 < 4

    def rows(idx):
        return pl.ds(pl.multiple_of(idx * c, c), c)

    def front(idx):
        sl = rows(idx)
        q = q_ref[sl, :]
        v = v_ref[sl, :]
        f = f_mid + f_half * jnp.tanh(0.5 * f_ref[sl, :])
        kk = 1.0 - f
        g = jnp.log2(f)

        cum, ends = [], []
        for i in range(nt):
            t = _tile(g, i)
            for k, m in scan_masks:
                t = t + pltpu.roll(t, k, 0) * m
            if i:
                t = t + ends[-1]
            cum.append(t)
            ends.append(_bcast_row(t, SUBLANES - 1))

        xs = []
        for w in widths:
            tiles = []
            for i in range(nt):
                qt, kt, ct = _tile(q, i), _tile(kk, i), cum[i]
                if w >= SUBLANES:
                    r = i * SUBLANES
                    mid = ends[(r - r % (2 * w) + w) // SUBLANES - 1]
                    x = qt * jnp.exp2(ct - mid) if r & w else kt * jnp.exp2(mid - ct)
                elif w == 4:
                    x = jnp.where(second[4], qt, kt) * jnp.exp2((ct - _bcast_row(ct, 3)) * sign[4])
                elif w == 2:
                    mid = jnp.where(low4, _bcast_row(ct, 1), _bcast_row(ct, 5))
                    x = jnp.where(second[2], qt, kt) * jnp.exp2((ct - mid) * sign[2])
                else:
                    x = jnp.where(second[1], qt * _tile(f, i), kt)
                tiles.append(x)
            xs.append(jnp.concatenate(tiles, axis=0).astype(BF16))

        last = ends[-1]
        qi = jnp.concatenate([_tile(q, i) * jnp.exp2(cum[i]) for i in range(nt)], axis=0)
        ks = jnp.concatenate([_tile(kk, i) * jnp.exp2(last - cum[i]) for i in range(nt)], axis=0)
        dec = jnp.exp2(last)
        od = jnp.sum(q * kk, axis=-1, keepdims=True) * v
        return xs, qi.astype(BF16), ks.astype(BF16), v.astype(BF16), dec, od

    def put_front(xs, qi, ks, vb, dec, od):
        for lvl in range(len(widths)):
            xs_ref[lvl] = xs[lvl]
        qi_ref[...] = qi
        ks_ref[...] = ks
        vb_ref[...] = vb
        dec_ref[...] = dec
        od_ref[...] = od

    def mid_start():
        a = [lax.dot_general(xs_ref[lvl], xs_ref[lvl], _NT, preferred_element_type=F32)
             for lvl in range(len(widths))]
        vb = vb_ref[...]
        upd = lax.dot_general(vb, ks_ref[...], _TN, preferred_element_type=F32)
        st = st_ref[...]
        o_inter = lax.dot_general(qi_ref[...], st.astype(BF16), _NT, preferred_element_type=F32)
        return st, vb, o_inter, upd, a

    def mid_finish(st, vb, o_inter, upd, a):
        tiles = []
        for i in range(nt):
            r = i * SUBLANES
            acc = None
            for lvl, w in enumerate(widths):
                if w >= SUBLANES and not r & w:
                    continue
                term = _tile(a[lvl], i) * masks_ref[lvl, r:r + SUBLANES, :]
                acc = term if acc is None else acc + term
            tiles.append(acc)
        sc_ref[...] = jnp.concatenate(tiles, axis=0).astype(BF16)
        op_ref[...] = o_inter + od_ref[...]
        vb2_ref[...] = vb
        st_ref[...] = st * dec_ref[0:1, :] + upd

    def back_start():
        return op_ref[...] + jnp.dot(sc_ref[...], vb2_ref[...], preferred_element_type=F32)

    def back_finish(o, idx):
        ms = jnp.mean(o * o, axis=-1, keepdims=True)
        y = o * lax.rsqrt(ms + EPS) * ng * _silu(gate_ref[rows(idx), :])
        o_ref[rows(idx), :] = y.astype(o_ref.dtype)

    def prologue():
        st_ref[...] = jnp.zeros_like(st_ref)
        sc_ref[...] = jnp.zeros_like(sc_ref)
        op_ref[...] = jnp.zeros_like(op_ref)
        vb2_ref[...] = jnp.zeros_like(vb2_ref)
        put_front(*front(0))

    def body(i):
        mid = mid_start()
        o_back = back_start()
        nxt = front(jnp.minimum(i + 1, n_chunks - 1))
        back_finish(o_back, jnp.maximum(i - 1, 0))
        mid_finish(*mid)
        put_front(*nxt)

    def epilogue():
        back_finish(back_start(), n_chunks - 1)

    return prologue, body, epilogue


def _hgrn_kernel(*refs, layer):
    prologue, body, epilogue = _hgrn_stages(*refs, layer)
    n_chunks = refs[0].shape[0] // HG_CHUNK
    prologue()

    def trip(i, carry):
        body(i)
        return carry

    lax.fori_loop(0, n_chunks, trip, 0, unroll=HG_UNROLL)
    epilogue()


def _mixers_inproj_kernel(*refs, layer):
    hg_in, lru_in, (h_ref, w_ref) = refs[:7], refs[7:14], refs[14:16]
    oa_ref, ob_ref, p_ref = refs[16:19]
    hg_scratch, (prev_ref, hl_ref) = refs[19:29], refs[29:31]
    part = pl.program_id(1)
    prologue, body, epilogue = _hgrn_stages(*hg_in, oa_ref, *hg_scratch, layer)
    lru_step = _lru_stage(*lru_in, ob_ref)
    lru_steps = HG_CHUNK * HG_UNROLL // LRU_ROWS

    @pl.when(part == 0)
    def _():
        prologue()
        prev_ref[...] = jnp.zeros_like(prev_ref)
        hl_ref[...] = jnp.zeros_like(hl_ref)

    pieces = 4
    rows_pp = h_ref.shape[0] // pieces
    carry = (prev_ref[...], hl_ref[...])
    for k in range(pieces):
        sl = slice(k * rows_pp, (k + 1) * rows_pp)
        p_ref[sl, :] = jnp.dot(h_ref[sl, :], w_ref[...], preferred_element_type=F32)
        for i in range(k * lru_steps // pieces, (k + 1) * lru_steps // pieces):
            carry = lru_step(part * lru_steps + i, carry)
        for i in range(k * HG_UNROLL // pieces, (k + 1) * HG_UNROLL // pieces):
            body(part * HG_UNROLL + i)
    prev_ref[...], hl_ref[...] = carry
    pl.when(part == pl.num_programs(1) - 1)(epilogue)


def _hgrn_in_specs(seq, depth, masks_shape, layer, head_of):
    nh = HG_HEADS

    def col(base):
        return pl.BlockSpec((seq, LANES), lambda *g: (0, base + head_of(*g)))

    return [
        col(0), col(nh), col(2 * nh), col(3 * nh),
        pl.BlockSpec((depth, LANES), lambda *g: (0, head_of(*g))),
        pl.BlockSpec((None, 1, LANES), lambda *g: (layer, 0, head_of(*g))),
        pl.BlockSpec(masks_shape, lambda *g: (0, 0, 0)),
    ]


def _hgrn_scratch(n_levels):
    c = HG_CHUNK
    return [
        pltpu.VMEM((HG_DV, HG_DK), F32),
        pltpu.VMEM((n_levels, c, HG_DK), BF16),
        pltpu.VMEM((c, HG_DK), BF16),
        pltpu.VMEM((c, HG_DK), BF16),
        pltpu.VMEM((c, HG_DV), BF16),
        pltpu.VMEM((SUBLANES, HG_DK), F32),
        pltpu.VMEM((c, HG_DV), F32),
        pltpu.VMEM((c, c), BF16),
        pltpu.VMEM((c, HG_DV), F32),
        pltpu.VMEM((c, HG_DV), BF16),
    ]


def _hgrn(proj, lb_param, ng, layer):
    masks = _hgrn_masks()
    seq = proj.shape[0]
    return pl.pallas_call(
        functools.partial(_hgrn_kernel, layer=layer),
        grid=(HG_HEADS,),
        in_specs=_hgrn_in_specs(seq, lb_param.shape[0], masks.shape, layer, lambda h: h),
        out_specs=pl.BlockSpec((seq, LANES), lambda h: (0, h)),
        out_shape=jax.ShapeDtypeStruct((seq, HG_HEADS * HG_DV), BF16),
        scratch_shapes=_hgrn_scratch(masks.shape[0]),
        compiler_params=pltpu.CompilerParams(
            dimension_semantics=("arbitrary",),
            vmem_limit_bytes=VMEM_LIMIT_BYTES),
        name="hgrn2",
    )(proj, proj, proj, proj, lb_param, ng, jnp.asarray(masks))


def _mixers_inproj(proj, lb_param, ng, lru_params, lru_cols, layer, h_src, h_block, w_bf16, layer_in):
    assert HG_HEADS == LRU_BLOCKS and (HG_CHUNK * HG_UNROLL) % LRU_ROWS == 0
    masks = _hgrn_masks()
    seq = proj.shape[0]
    d = h_src.shape[1]
    n = w_bf16.shape[2]
    parts = seq // (HG_CHUNK * HG_UNROLL)
    tn = n // (HG_HEADS * parts)
    col_out = pl.BlockSpec((seq, LANES), lambda j, p: (0, j))
    return pl.pallas_call(
        functools.partial(_mixers_inproj_kernel, layer=layer),
        grid=(HG_HEADS, parts),
        in_specs=(_hgrn_in_specs(seq, lb_param.shape[0], masks.shape, layer, lambda j, p: j)
                  + _lru_in_specs(seq, layer, *lru_cols, lambda j, p: j) + [
                      pl.BlockSpec((seq, d), lambda j, p: (h_block, 0)),
                      pl.BlockSpec((None, d, tn), lambda j, p: (layer_in, 0, j * parts + p)),
                  ]),
        out_specs=[col_out, col_out, pl.BlockSpec((seq, tn), lambda j, p: (0, j * parts + p))],
        out_shape=[
            jax.ShapeDtypeStruct((seq, HG_HEADS * HG_DV), BF16),
            jax.ShapeDtypeStruct((seq, LRU_BLOCKS * LRU_BLOCK), BF16),
            jax.ShapeDtypeStruct((seq, n), F32),
        ],
        scratch_shapes=_hgrn_scratch(masks.shape[0]) + [
            pltpu.VMEM((SUBLANES, LANES), F32),
            pltpu.VMEM((SUBLANES, LANES), F32),
        ],
        compiler_params=pltpu.CompilerParams(
            dimension_semantics=("arbitrary", "arbitrary"),
            vmem_limit_bytes=VMEM_LIMIT_BYTES),
        name="mixers_inproj",
    )(proj, proj, proj, proj, lb_param, ng, jnp.asarray(masks), proj, proj, *lru_params,
      h_src, w_bf16)


def _lru_stage(x_ref, gate_ref, cw_ref, cb_ref, wri_ref, bri_ref, lam_ref, o_ref):
    rws = LRU_ROWS
    nt = rws // SUBLANES
    cw = cw_ref[...]
    cb = cb_ref[...]
    bri = bri_ref[...]
    lam = lam_ref[...]
    log_a_scale = (-LRU_C) * (jnp.log1p(jnp.exp(-jnp.abs(lam))) + jnp.maximum(-lam, 0.0))
    tap = [cw[CONV_WIDTH - 1 - s:CONV_WIDTH - s, :] for s in range(CONV_WIDTH)]
    row8 = lax.broadcasted_iota(jnp.int32, (SUBLANES, LANES), 0)
    from_prev = [row8 < s for s in range(CONV_WIDTH)]
    scan_masks = [(k, row8 >= k) for k in (1, 2, 4)]

    def step(n, carry):
        prev8, hlast = carry
        r0 = pl.multiple_of(n * rws, rws)
        x = x_ref[pl.ds(r0, rws), :]

        prev_rolled = [pltpu.roll(prev8, s, 0) for s in range(1, CONV_WIDTH)]
        xc_tiles = []
        for j in range(nt):
            t = _tile(x, j)
            rolled = [pltpu.roll(t, s, 0) for s in range(1, CONV_WIDTH)]
            acc = t * tap[0] + cb
            for s in range(1, CONV_WIDTH):
                acc = acc + jnp.where(from_prev[s], prev_rolled[s - 1], rolled[s - 1]) * tap[s]
            prev_rolled = rolled
            xc_tiles.append(acc)
        xc = jnp.concatenate(xc_tiles, axis=0)

        ri = jnp.dot(xc.astype(BF16), wri_ref[...], preferred_element_type=F32) + bri
        log_a = _sigmoid(ri[:, :LANES]) * log_a_scale
        a = jnp.exp(log_a)
        u = jnp.sqrt(-jnp.tanh(log_a) * (a * a + 1.0)) * (_sigmoid(ri[:, LANES:]) * xc)

        hs = []
        for j in range(nt):
            at, ut = _tile(a, j), _tile(u, j)
            for k, m in scan_masks:
                ut = ut + at * jnp.where(m, pltpu.roll(ut, k, 0), 0.0)
                at = at * jnp.where(m, pltpu.roll(at, k, 0), 1.0)
            h8 = at * hlast + ut
            hlast = _bcast_row(h8, SUBLANES - 1)
            hs.append(h8)
        h = jnp.concatenate(hs, axis=0)

        y = h * _silu(gate_ref[pl.ds(r0, rws), :])
        o_ref[pl.ds(r0, rws), :] = y.astype(o_ref.dtype)
        return _tile(x, nt - 1), hlast

    return step


def _lru_kernel(*refs):
    step = _lru_stage(*refs)
    zeros8 = jnp.zeros((SUBLANES, LANES), F32)
    lax.fori_loop(0, refs[0].shape[0] // LRU_ROWS, step, (zeros8, zeros8))


def _lru_in_specs(seq, layer, x_col, gate_col, block_of):
    def col(base):
        return pl.BlockSpec((seq, LANES), lambda *g: (0, base + block_of(*g)))

    def vec(rows, width=LANES):
        return pl.BlockSpec((None, rows, width), lambda *g: (layer, 0, block_of(*g)))

    return [
        col(x_col), col(gate_col),
        vec(CONV_WIDTH), vec(1),
        pl.BlockSpec((None, None, LRU_BLOCK, 2 * LRU_BLOCK), lambda *g: (layer, block_of(*g), 0, 0)),
        vec(1, 2 * LANES),
        vec(1),
    ]


def _lru(proj, lru_params, lru_cols, layer):
    seq = proj.shape[0]
    return pl.pallas_call(
        _lru_kernel,
        grid=(LRU_BLOCKS,),
        in_specs=_lru_in_specs(seq, layer, *lru_cols, lambda n: n),
        out_specs=pl.BlockSpec((seq, LANES), lambda n: (0, n)),
        out_shape=jax.ShapeDtypeStruct((seq, LRU_BLOCKS * LRU_BLOCK), BF16),
        compiler_params=pltpu.CompilerParams(
            dimension_semantics=("parallel",),
            vmem_limit_bytes=VMEM_LIMIT_BYTES),
        name="rglru",
    )(proj, proj, *lru_params)


def _kv_kernel(mem_ref, g_ref, w_ref, o_ref, h_ref):
    @pl.when(pl.program_id(1) == 0)
    def _():
        h_ref[...] = _rmsnorm_rows(mem_ref[...], g_ref[...]).astype(BF16)

    o_ref[...] = jnp.dot(h_ref[...], w_ref[...], preferred_element_type=F32).astype(o_ref.dtype)


def _kv(mem2, g, w_bf16, tn=512):
    depth, d, n = w_bf16.shape
    m = mem2.shape[0]
    return pl.pallas_call(
        _kv_kernel,
        grid=(depth, n // tn),
        in_specs=[
            pl.BlockSpec((m, d), lambda l, j: (0, 0)),
            pl.BlockSpec((None, 1, d), lambda l, j: (l, 0, 0)),
            pl.BlockSpec((None, d, tn), lambda l, j: (l, 0, j)),
        ],
        out_specs=pl.BlockSpec((None, m, tn), lambda l, j: (l, 0, j)),
        out_shape=jax.ShapeDtypeStruct((depth, m, n), BF16),
        scratch_shapes=[pltpu.VMEM((m, d), BF16)],
        compiler_params=pltpu.CompilerParams(
            dimension_semantics=("parallel", "arbitrary"),
            vmem_limit_bytes=VMEM_LIMIT_BYTES),
        name="mem_kv",
    )(mem2, g, w_bf16)


def _attn_kernel(q_ref, gate_ref, k_ref, v_ref, o_ref):
    scale = XA_HEAD_DIM ** -0.5
    for h in range(XA_HEADS):
        sl = slice(h * XA_HEAD_DIM, (h + 1) * XA_HEAD_DIM)
        s = lax.dot_general(q_ref[:, sl].astype(BF16), k_ref[:, sl], _NT,
                            preferred_element_type=F32) * scale
        m = jnp.max(s, axis=-1, keepdims=True)
        p = jnp.exp(s - m)
        p = p / jnp.sum(p, axis=-1, keepdims=True)
        o = jnp.dot(p.astype(BF16), v_ref[:, sl], preferred_element_type=F32)
        o_ref[:, sl] = (o * _silu(gate_ref[:, sl])).astype(o_ref.dtype)


def _attn(proj, kv, kv_block, n_mem, q_col, gate_col, ts=512):
    seq = proj.shape[0]
    width = XA_HEADS * XA_HEAD_DIM
    return pl.pallas_call(
        _attn_kernel,
        grid=(seq // ts,),
        in_specs=[
            pl.BlockSpec((ts, width), lambda i: (i, q_col)),
            pl.BlockSpec((ts, width), lambda i: (i, gate_col)),
            pl.BlockSpec((None, n_mem, width), lambda i: (kv_block, 0, 0)),
            pl.BlockSpec((None, n_mem, width), lambda i: (kv_block, 0, 1)),
        ],
        out_specs=pl.BlockSpec((ts, width), lambda i: (i, 0)),
        out_shape=jax.ShapeDtypeStruct((seq, width), BF16),
        compiler_params=pltpu.CompilerParams(
            dimension_semantics=("parallel",),
            vmem_limit_bytes=VMEM_LIMIT_BYTES),
        name="mem_attn",
    )(proj, proj, kv, kv)


def _outproj_kernel(x_ref, a_ref, b_ref, c_ref, w_ref, g_ref, *out_refs):
    ka = a_ref.shape[1]
    kb = b_ref.shape[1]
    acc = x_ref[...]
    acc = acc + jnp.dot(a_ref[...], w_ref[0:ka, :], preferred_element_type=F32)
    acc = acc + jnp.dot(b_ref[...], w_ref[ka:ka + kb, :], preferred_element_type=F32)
    acc = acc + jnp.dot(c_ref[...], w_ref[ka + kb:, :], preferred_element_type=F32)
    y_ref = out_refs[-1]
    y_ref[...] = _rmsnorm_rows(acc, g_ref[...]).astype(y_ref.dtype)
    if len(out_refs) == 2:
        out_refs[0][...] = acc


def _outproj(x_src, x_block, oa, ob, oc, w_bf16, layer, g3, g_layer, last, tm=512):
    seq = oa.shape[0]
    d = x_src.shape[1]
    k = w_bf16.shape[1]
    nt = seq // tm

    def lhs(arr):
        return pl.BlockSpec((tm, arr.shape[1]), lambda i: (i, 0))

    row_block = pl.BlockSpec((tm, d), lambda i: (i, 0))
    if last:
        out_specs = [row_block]
        out_shape = [jax.ShapeDtypeStruct((seq, d), F32)]
    else:
        out_specs = [row_block, row_block]
        out_shape = [jax.ShapeDtypeStruct((seq, d), F32), jax.ShapeDtypeStruct((seq, d), BF16)]
    return pl.pallas_call(
        _outproj_kernel,
        grid=(nt,),
        in_specs=[
            pl.BlockSpec((tm, d), lambda i: (x_block * nt + i, 0)),
            lhs(oa), lhs(ob), lhs(oc),
            pl.BlockSpec((None, k, d), lambda i: (layer, 0, 0), pipeline_mode=pl.Buffered(1)),
            pl.BlockSpec((None, 1, d), lambda i: (g_layer, 0, 0)),
        ],
        out_specs=out_specs,
        out_shape=out_shape,
        compiler_params=pltpu.CompilerParams(
            dimension_semantics=("parallel",),
            vmem_limit_bytes=VMEM_LIMIT_BYTES),
        name="outproj",
    )(x_src, oa, ob, oc, w_bf16, g3)


def kernel(x, mem, norm_g, w_in, lb_param, hg_norm_g, conv_w, conv_b, w_r, b_r, w_i, b_i, lam,
           mem_norm_g, w_kv, w_out, final_g):
    batch, seq, d_model = x.shape
    depth = w_in.shape[0]
    n_mem = mem.shape[1]
    hg_key = HG_HEADS * HG_DK
    hg_val = HG_HEADS * HG_DV
    lru_w = LRU_BLOCKS * LRU_BLOCK
    xa_w = XA_HEADS * XA_HEAD_DIM
    assert seq % (HG_CHUNK * HG_UNROLL) == 0 and seq % LRU_ROWS == 0
    assert w_in.shape[2] == 2 * hg_key + 2 * hg_val + 2 * lru_w + 2 * xa_w
    assert hg_key == hg_val == lru_w == xa_w
    lru_x_col = (2 * hg_key + 2 * hg_val) // LANES
    lru_gate_col = lru_x_col + lru_w // LANES
    xa_q_col = (2 * hg_key + 2 * hg_val + 2 * lru_w) // xa_w
    xa_gate_col = xa_q_col + 1

    w_in_b = w_in.astype(BF16)
    w_out_b = w_out.astype(BF16)
    w_kv_b = w_kv.astype(BF16)
    wri = jnp.concatenate([w_r, w_i], axis=-1).astype(BF16)
    bri = jnp.concatenate([b_r, b_i], axis=-1).reshape(depth, 1, -1)
    post_g3 = jnp.concatenate([norm_g[1:], final_g[None]], axis=0).reshape(depth, 1, d_model)
    norm_g3 = norm_g.reshape(depth, 1, d_model)
    hg_norm_g3 = hg_norm_g.reshape(depth, 1, hg_val)
    conv_b3 = conv_b.reshape(depth, 1, lru_w)
    lam3 = lam.reshape(depth, 1, lru_w)

    x2 = x.reshape(batch * seq, d_model)
    kv = _kv(mem.reshape(batch * n_mem, d_model), mem_norm_g.reshape(depth, 1, d_model), w_kv_b)
    kv = kv.reshape(depth * batch, n_mem, 2 * xa_w)

    units = [(l, b) for l in range(depth) for b in range(batch)]
    lru_params = (conv_w, conv_b3, wri, bri, lam3)
    lru_cols = (lru_x_col, lru_gate_col)
    h0 = _norm(x2, norm_g3, 0, BF16)
    resid = {(0, b): (x2, b) for b in range(batch)}
    normed = {(0, b): (h0, b) for b in range(batch)}
    proj = _inproj(*normed[units[0]], w_in_b, 0, seq)
    finals = []
    for u, (l, b) in enumerate(units):
        if u + 1 < len(units):
            nxt = units[u + 1]
            o_a, o_b, proj_next = _mixers_inproj(proj, lb_param, hg_norm_g3, lru_params, lru_cols, l,
                                                 *normed[nxt], w_in_b, nxt[0])
        else:
            o_a = _hgrn(proj, lb_param, hg_norm_g3, l)
            o_b = _lru(proj, lru_params, lru_cols, l)
            proj_next = None
        o_c = _attn(proj, kv, l * batch + b, n_mem, xa_q_col, xa_gate_col)
        last = l + 1 == depth
        outs = _outproj(*resid[(l, b)], o_a, o_b, o_c, w_out_b, l, post_g3, l, last)
        if last:
            finals.append(outs[0])
        else:
            resid[(l + 1, b)] = (outs[0], 0)
            normed[(l + 1, b)] = (outs[1], 0)
        proj = proj_next

    return jnp.stack(finals, axis=0)
```

```python
import functools

import numpy as np
import jax
import jax.numpy as jnp
from jax import lax
from jax.experimental import pallas as pl
from jax.experimental.pallas import tpu as pltpu

F32 = jnp.float32
BF16 = jnp.bfloat16

EPS = 1e-6
LRU_C = 8.0
CONV_WIDTH = 4

HG_HEADS = 8
HG_DK = 128
HG_DV = 128
LRU_BLOCKS = 8
LRU_BLOCK = 128
XA_HEADS = 4
XA_HEAD_DIM = 256

SUBLANES = 8
LANES = 128
VMEM_LIMIT_BYTES = 56 * 1024 * 1024

HG_CHUNK = 64
HG_UNROLL = 8
LRU_ROWS = 256


def _half_tanh(x):
    h = 0.5 * x
    return h, jnp.tanh(h)


def _sigmoid(x):
    _, th = _half_tanh(x)
    return 0.5 + 0.5 * th


def _silu(x):
    h, th = _half_tanh(x)
    return h + h * th


def _tile(x, i):
    return x[i * SUBLANES:(i + 1) * SUBLANES]


def _bcast_row(tile, r):
    return jnp.broadcast_to(tile[r:r + 1, :], tile.shape)


def _rmsnorm_rows(x, g):
    ms = jnp.mean(x * x, axis=-1, keepdims=True)
    return x * lax.rsqrt(ms + EPS) * g


def _norm_kernel(x_ref, g_ref, o_ref):
    o_ref[...] = _rmsnorm_rows(x_ref[...], g_ref[...]).astype(o_ref.dtype)


def _norm(x2, g3, layer, dtype, tm=512):
    t, d = x2.shape
    return pl.pallas_call(
        _norm_kernel,
        grid=(t // tm,),
        in_specs=[pl.BlockSpec((tm, d), lambda i: (i, 0)),
                  pl.BlockSpec((None, 1, d), lambda i: (layer, 0, 0))],
        out_specs=pl.BlockSpec((tm, d), lambda i: (i, 0)),
        out_shape=jax.ShapeDtypeStruct((t, d), dtype),
        compiler_params=pltpu.CompilerParams(
            dimension_semantics=("parallel",),
            vmem_limit_bytes=VMEM_LIMIT_BYTES),
        name="norm",
    )(x2, g3)


def _inproj_kernel(h_ref, w_ref, o_ref):
    acc = jnp.dot(h_ref[...], w_ref[...].astype(BF16), preferred_element_type=F32)
    for c in range(o_ref.shape[0]):
        o_ref[c] = acc[:, c * LANES:(c + 1) * LANES]


def _inproj(h, w_in, layer, tm=2048, tn=512):
    t, d = h.shape
    n = w_in.shape[2]
    return pl.pallas_call(
        _inproj_kernel,
        grid=(t // tm, n // tn),
        in_specs=[
            pl.BlockSpec((tm, d), lambda i, j: (i, 0)),
            pl.BlockSpec((None, d, tn), lambda i, j: (layer, 0, j)),
        ],
        out_specs=pl.BlockSpec((tn // LANES, tm, LANES), lambda i, j: (j, i, 0)),
        out_shape=jax.ShapeDtypeStruct((n // LANES, t, LANES), F32),
        compiler_params=pltpu.CompilerParams(
            dimension_semantics=("parallel", "arbitrary"),
            vmem_limit_bytes=VMEM_LIMIT_BYTES),
        name="inproj",
    )(h, w_in)


def _hgrn_level_widths():
    w = HG_CHUNK // 2
    widths = []
    while w >= 1:
        widths.append(w)
        w //= 2
    return widths


def _hgrn_masks():
    t = np.arange(HG_CHUNK)[:, None]
    s = np.arange(HG_CHUNK)[None, :]
    masks = [((t // (2 * w)) == (s // (2 * w))) & ((t % (2 * w)) >= w) & ((s % (2 * w)) < w)
             for w in _hgrn_level_widths()]
    return np.stack(masks).astype(np.float32)


_NT = (((1,), (1,)), ((), ()))
_TN = (((0,), (0,)), ((), ()))


def _hgrn_kernel(q_ref, f_ref, v_ref, gate_ref, lbp_ref, ng_ref, masks_ref, o_ref,
                 st_ref, xs_ref, qi_ref, ks_ref, vb_ref, dec_ref, od_ref, sc_ref, op_ref, vb2_ref,
                 *, layer):
    c = HG_CHUNK
    nt = c // SUBLANES
    widths = _hgrn_level_widths()
    n_chunks = q_ref.shape[0] // c
    lbp = lbp_ref[...]
    lbe = jnp.exp(lbp - jnp.max(lbp, axis=0, keepdims=True))
    sm = lbe / jnp.sum(lbe, axis=0, keepdims=True)
    lb = jnp.sum(sm[1:layer + 1], axis=0, keepdims=True) if layer else jnp.zeros((1, HG_DK), F32)
    f_mid = 0.5 * (1.0 + lb)
    f_half = 0.5 * (1.0 - lb)
    ng = ng_ref[...]

    row8 = lax.broadcasted_iota(jnp.int32, (SUBLANES, LANES), 0)
    scan_masks = [(k, (row8 >= k).astype(F32)) for k in (1, 2, 4)]
    second = {w: (row8 & w) != 0 for w in (4, 2, 1)}
    sign = {w: jnp.where(second[w], 1.0, -1.0) for w in (4, 2)}
    low4 = row8 < 4

    def rows(idx):
        return pl.ds(pl.multiple_of(idx * c, c), c)

    def front(idx):
        sl = rows(idx)
        q = q_ref[sl, :]
        v = v_ref[sl, :]
        f = f_mid + f_half * jnp.tanh(0.5 * f_ref[sl, :])
        kk = 1.0 - f
        g = jnp.log2(f)

        cum, ends = [], []
        for i in range(nt):
            t = _tile(g, i)
            for k, m in scan_masks:
                t = t + pltpu.roll(t, k, 0) * m
            if i:
                t = t + ends[-1]
            cum.append(t)
            ends.append(_bcast_row(t, SUBLANES - 1))

        xs = []
        for w in widths:
            tiles = []
            for i in range(nt):
                qt, kt, ct = _tile(q, i), _tile(kk, i), cum[i]
                if w >= SUBLANES:
                    r = i * SUBLANES
                    mid = ends[(r - r % (2 * w) + w) // SUBLANES - 1]
                    x = qt * jnp.exp2(ct - mid) if r & w else kt * jnp.exp2(mid - ct)
                elif w == 4:
                    x = jnp.where(second[4], qt, kt) * jnp.exp2((ct - _bcast_row(ct, 3)) * sign[4])
                elif w == 2:
                    mid = jnp.where(low4, _bcast_row(ct, 1), _bcast_row(ct, 5))
                    x = jnp.where(second[2], qt, kt) * jnp.exp2((ct - mid) * sign[2])
                else:
                    x = jnp.where(second[1], qt * _tile(f, i), kt)
                tiles.append(x)
            xs.append(jnp.concatenate(tiles, axis=0).astype(BF16))

        last = ends[-1]
        qi = jnp.concatenate([_tile(q, i) * jnp.exp2(cum[i]) for i in range(nt)], axis=0)
        ks = jnp.concatenate([_tile(kk, i) * jnp.exp2(last - cum[i]) for i in range(nt)], axis=0)
        dec = jnp.exp2(last)
        od = jnp.sum(q * kk, axis=-1, keepdims=True) * v
        return xs, qi.astype(BF16), ks.astype(BF16), v.astype(BF16), dec, od

    def put_front(xs, qi, ks, vb, dec, od):
        for lvl in range(len(widths)):
            xs_ref[lvl] = xs[lvl]
        qi_ref[...] = qi
        ks_ref[...] = ks
        vb_ref[...] = vb
        dec_ref[...] = dec
        od_ref[...] = od

    def mid_start():
        a = [lax.dot_general(xs_ref[lvl], xs_ref[lvl], _NT, preferred_element_type=F32)
             for lvl in range(len(widths))]
        vb = vb_ref[...]
        upd = lax.dot_general(vb, ks_ref[...], _TN, preferred_element_type=F32)
        st = st_ref[...]
        o_inter = lax.dot_general(qi_ref[...], st.astype(BF16), _NT, preferred_element_type=F32)
        return st, vb, o_inter, upd, a

    def mid_finish(st, vb, o_inter, upd, a):
        tiles = []
        for i in range(nt):
            r = i * SUBLANES
            acc = None
            for lvl, w in enumerate(widths):
                if w >= SUBLANES and not r & w:
                    continue
                term = _tile(a[lvl], i) * masks_ref[lvl, r:r + SUBLANES, :]
                acc = term if acc is None else acc + term
            tiles.append(acc)
        sc_ref[...] = jnp.concatenate(tiles, axis=0).astype(BF16)
        op_ref[...] = o_inter + od_ref[...]
        vb2_ref[...] = vb
        st_ref[...] = st * dec_ref[0:1, :] + upd

    def back_start():
        return op_ref[...] + jnp.dot(sc_ref[...], vb2_ref[...], preferred_element_type=F32)

    def back_finish(o, idx):
        ms = jnp.mean(o * o, axis=-1, keepdims=True)
        y = o * lax.rsqrt(ms + EPS) * ng * _silu(gate_ref[rows(idx), :])
        o_ref[rows(idx), :] = y.astype(o_ref.dtype)

    st_ref[...] = jnp.zeros_like(st_ref)
    sc_ref[...] = jnp.zeros_like(sc_ref)
    op_ref[...] = jnp.zeros_like(op_ref)
    vb2_ref[...] = jnp.zeros_like(vb2_ref)
    put_front(*front(0))

    def body(i, carry):
        mid = mid_start()
        o_back = back_start()
        nxt = front(jnp.minimum(i + 1, n_chunks - 1))
        back_finish(o_back, jnp.maximum(i - 1, 0))
        mid_finish(*mid)
        put_front(*nxt)
        return carry

    lax.fori_loop(0, n_chunks, body, 0, unroll=HG_UNROLL)
    back_finish(back_start(), n_chunks - 1)


def _hgrn(proj, lb_param, ng, layer, batch, seq):
    masks = _hgrn_masks()
    c = HG_CHUNK
    t = proj.shape[1]
    nh = HG_HEADS
    depth = lb_param.shape[0]

    def col(base):
        return pl.BlockSpec((None, seq, LANES), lambda b, h: (base + h, b, 0))

    return pl.pallas_call(
        functools.partial(_hgrn_kernel, layer=layer),
        grid=(batch, nh),
        in_specs=[
            col(0), col(nh), col(2 * nh), col(3 * nh),
            pl.BlockSpec((depth, LANES), lambda b, h: (0, h)),
            pl.BlockSpec((None, 1, LANES), lambda b, h: (layer, 0, h)),
            pl.BlockSpec(masks.shape, lambda b, h: (0, 0, 0)),
        ],
        out_specs=pl.BlockSpec((seq, LANES), lambda b, h: (b, h)),
        out_shape=jax.ShapeDtypeStruct((t, nh * HG_DV), BF16),
        scratch_shapes=[
            pltpu.VMEM((HG_DV, HG_DK), F32),
            pltpu.VMEM((masks.shape[0], c, HG_DK), BF16),
            pltpu.VMEM((c, HG_DK), BF16),
            pltpu.VMEM((c, HG_DK), BF16),
            pltpu.VMEM((c, HG_DV), BF16),
            pltpu.VMEM((SUBLANES, HG_DK), F32),
            pltpu.VMEM((c, HG_DV), F32),
            pltpu.VMEM((c, c), BF16),
            pltpu.VMEM((c, HG_DV), F32),
            pltpu.VMEM((c, HG_DV), BF16),
        ],
        compiler_params=pltpu.CompilerParams(
            dimension_semantics=("parallel", "parallel"),
            vmem_limit_bytes=VMEM_LIMIT_BYTES),
        name="hgrn2",
    )(proj, proj, proj, proj, lb_param, ng, jnp.asarray(masks))


def _lru_kernel(x_ref, gate_ref, cw_ref, cb_ref, wri_ref, bri_ref, lam_ref, o_ref):
    rws = LRU_ROWS
    nt = rws // SUBLANES
    n_steps = x_ref.shape[0] // rws
    cw = cw_ref[...]
    cb = cb_ref[...]
    bri = bri_ref[...]
    lam = lam_ref[...]
    log_a_scale = (-LRU_C) * (jnp.log1p(jnp.exp(-jnp.abs(lam))) + jnp.maximum(-lam, 0.0))
    tap = [cw[CONV_WIDTH - 1 - s:CONV_WIDTH - s, :] for s in range(CONV_WIDTH)]
    row8 = lax.broadcasted_iota(jnp.int32, (SUBLANES, LANES), 0)
    from_prev = [row8 < s for s in range(CONV_WIDTH)]
    scan_masks = [(k, row8 >= k) for k in (1, 2, 4)]

    def step(n, carry):
        prev8, hlast = carry
        r0 = pl.multiple_of(n * rws, rws)
        x = x_ref[pl.ds(r0, rws), :]

        prev_rolled = [pltpu.roll(prev8, s, 0) for s in range(1, CONV_WIDTH)]
        xc_tiles = []
        for j in range(nt):
            t = _tile(x, j)
            rolled = [pltpu.roll(t, s, 0) for s in range(1, CONV_WIDTH)]
            acc = t * tap[0] + cb
            for s in range(1, CONV_WIDTH):
                acc = acc + jnp.where(from_prev[s], prev_rolled[s - 1], rolled[s - 1]) * tap[s]
            prev_rolled = rolled
            xc_tiles.append(acc)
        xc = jnp.concatenate(xc_tiles, axis=0)

        ri = jnp.dot(xc.astype(BF16), wri_ref[...], preferred_element_type=F32) + bri
        log_a = _sigmoid(ri[:, :LANES]) * log_a_scale
        a = jnp.exp(log_a)
        u = jnp.sqrt(-jnp.tanh(log_a) * (a * a + 1.0)) * (_sigmoid(ri[:, LANES:]) * xc)

        hs = []
        for j in range(nt):
            at, ut = _tile(a, j), _tile(u, j)
            for k, m in scan_masks:
                ut = ut + at * jnp.where(m, pltpu.roll(ut, k, 0), 0.0)
                at = at * jnp.where(m, pltpu.roll(at, k, 0), 1.0)
            h8 = at * hlast + ut
            hlast = _bcast_row(h8, SUBLANES - 1)
            hs.append(h8)
        h = jnp.concatenate(hs, axis=0)

        y = h * _silu(gate_ref[pl.ds(r0, rws), :])
        o_ref[pl.ds(r0, rws), :] = y.astype(o_ref.dtype)
        return _tile(x, nt - 1), hlast

    zeros8 = jnp.zeros((SUBLANES, LANES), F32)
    lax.fori_loop(0, n_steps, step, (zeros8, zeros8))


def _lru(proj, cw, cb, wri, bri, lam, layer, batch, seq, x_col, gate_col):
    t = proj.shape[1]
    nb = LRU_BLOCKS

    def col(base):
        return pl.BlockSpec((None, seq, LANES), lambda b, n: (base + n, b, 0))

    def vec(rows, width=LANES):
        return pl.BlockSpec((None, rows, width), lambda b, n: (layer, 0, n))

    return pl.pallas_call(
        _lru_kernel,
        grid=(batch, nb),
        in_specs=[
            col(x_col), col(gate_col),
            vec(CONV_WIDTH), vec(1),
            pl.BlockSpec((None, None, LRU_BLOCK, 2 * LRU_BLOCK), lambda b, n: (layer, n, 0, 0)),
            vec(1, 2 * LANES),
            vec(1),
        ],
        out_specs=pl.BlockSpec((seq, LANES), lambda b, n: (b, n)),
        out_shape=jax.ShapeDtypeStruct((t, nb * LRU_BLOCK), BF16),
        compiler_params=pltpu.CompilerParams(
            dimension_semantics=("parallel", "parallel"),
            vmem_limit_bytes=VMEM_LIMIT_BYTES),
        name="rglru",
    )(proj, proj, cw, cb, wri, bri, lam)


def _kv_kernel(mem_ref, g_ref, w_ref, o_ref, h_ref):
    @pl.when(pl.program_id(1) == 0)
    def _():
        h_ref[...] = _rmsnorm_rows(mem_ref[...], g_ref[...]).astype(BF16)

    o_ref[...] = jnp.dot(h_ref[...], w_ref[...], preferred_element_type=F32).astype(o_ref.dtype)


def _kv(mem2, g, w_bf16, tn=512):
    depth, d, n = w_bf16.shape
    m = mem2.shape[0]
    return pl.pallas_call(
        _kv_kernel,
        grid=(depth, n // tn),
        in_specs=[
            pl.BlockSpec((m, d), lambda l, j: (0, 0)),
            pl.BlockSpec((None, 1, d), lambda l, j: (l, 0, 0)),
            pl.BlockSpec((None, d, tn), lambda l, j: (l, 0, j)),
        ],
        out_specs=pl.BlockSpec((None, m, tn), lambda l, j: (l, 0, j)),
        out_shape=jax.ShapeDtypeStruct((depth, m, n), BF16),
        scratch_shapes=[pltpu.VMEM((m, d), BF16)],
        compiler_params=pltpu.CompilerParams(
            dimension_semantics=("parallel", "arbitrary"),
            vmem_limit_bytes=VMEM_LIMIT_BYTES),
        name="mem_kv",
    )(mem2, g, w_bf16)


def _attn_kernel(q_ref, gate_ref, k_ref, v_ref, o_ref):
    scale = XA_HEAD_DIM ** -0.5
    per_head = XA_HEAD_DIM // LANES

    def head_cols(ref, h):
        return jnp.concatenate([ref[h * per_head + c] for c in range(per_head)], axis=1)

    for h in range(XA_HEADS):
        sl = slice(h * XA_HEAD_DIM, (h + 1) * XA_HEAD_DIM)
        s = lax.dot_general(head_cols(q_ref, h).astype(BF16), k_ref[:, sl], _NT,
                            preferred_element_type=F32) * scale
        m = jnp.max(s, axis=-1, keepdims=True)
        p = jnp.exp(s - m)
        p = p / jnp.sum(p, axis=-1, keepdims=True)
        o = jnp.dot(p.astype(BF16), v_ref[:, sl], preferred_element_type=F32)
        o_ref[:, sl] = (o * _silu(head_cols(gate_ref, h))).astype(o_ref.dtype)


def _attn(proj, kv, layer, batch, seq, n_mem, q_col, gate_col, ts=512):
    t = proj.shape[1]
    width = XA_HEADS * XA_HEAD_DIM
    nt = seq // ts
    cols = width // LANES
    return pl.pallas_call(
        _attn_kernel,
        grid=(batch, nt),
        in_specs=[
            pl.BlockSpec((cols, ts, LANES), lambda b, i: (q_col, b * nt + i, 0)),
            pl.BlockSpec((cols, ts, LANES), lambda b, i: (gate_col, b * nt + i, 0)),
            pl.BlockSpec((None, n_mem, width), lambda b, i: (layer * batch + b, 0, 0)),
            pl.BlockSpec((None, n_mem, width), lambda b, i: (layer * batch + b, 0, 1)),
        ],
        out_specs=pl.BlockSpec((ts, width), lambda b, i: (b * nt + i, 0)),
        out_shape=jax.ShapeDtypeStruct((t, width), BF16),
        compiler_params=pltpu.CompilerParams(
            dimension_semantics=("parallel", "parallel"),
            vmem_limit_bytes=VMEM_LIMIT_BYTES),
        name="mem_attn",
    )(proj, proj, kv, kv)


def _outproj_kernel(x_ref, a_ref, b_ref, c_ref, w_ref, g_ref, *out_refs):
    ka = a_ref.shape[1]
    kb = b_ref.shape[1]
    acc = x_ref[...]
    acc = acc + jnp.dot(a_ref[...], w_ref[0:ka, :], preferred_element_type=F32)
    acc = acc + jnp.dot(b_ref[...], w_ref[ka:ka + kb, :], preferred_element_type=F32)
    acc = acc + jnp.dot(c_ref[...], w_ref[ka + kb:, :], preferred_element_type=F32)
    y_ref = out_refs[-1]
    y_ref[...] = _rmsnorm_rows(acc, g_ref[...]).astype(y_ref.dtype)
    if len(out_refs) == 2:
        out_refs[0][...] = acc


def _outproj(x2, oa, ob, oc, w_bf16, layer, g3, last, tm=512):
    t, d = x2.shape
    k = w_bf16.shape[1]

    def lhs(arr):
        return pl.BlockSpec((tm, arr.shape[1]), lambda i: (i, 0))

    row_block = pl.BlockSpec((tm, d), lambda i: (i, 0))
    if last:
        out_specs = [row_block]
        out_shape = [jax.ShapeDtypeStruct((t, d), F32)]
    else:
        out_specs = [row_block, row_block]
        out_shape = [jax.ShapeDtypeStruct((t, d), F32), jax.ShapeDtypeStruct((t, d), BF16)]
    return pl.pallas_call(
        _outproj_kernel,
        grid=(t // tm,),
        in_specs=[
            row_block,
            lhs(oa), lhs(ob), lhs(oc),
            pl.BlockSpec((None, k, d), lambda i: (layer, 0, 0), pipeline_mode=pl.Buffered(1)),
            pl.BlockSpec((None, 1, d), lambda i: (layer, 0, 0)),
        ],
        out_specs=out_specs,
        out_shape=out_shape,
        compiler_params=pltpu.CompilerParams(
            dimension_semantics=("parallel",),
            vmem_limit_bytes=VMEM_LIMIT_BYTES),
        name="outproj",
    )(x2, oa, ob, oc, w_bf16, g3)


def kernel(x, mem, norm_g, w_in, lb_param, hg_norm_g, conv_w, conv_b, w_r, b_r, w_i, b_i, lam,
           mem_norm_g, w_kv, w_out, final_g):
    batch, seq, d_model = x.shape
    depth = w_in.shape[0]
    n_mem = mem.shape[1]
    hg_key = HG_HEADS * HG_DK
    hg_val = HG_HEADS * HG_DV
    lru_w = LRU_BLOCKS * LRU_BLOCK
    xa_w = XA_HEADS * XA_HEAD_DIM
    assert seq % (HG_CHUNK * HG_UNROLL) == 0 and seq % LRU_ROWS == 0
    assert w_in.shape[2] == 2 * hg_key + 2 * hg_val + 2 * lru_w + 2 * xa_w
    assert hg_key == hg_val == lru_w == xa_w
    lru_x_col = (2 * hg_key + 2 * hg_val) // LANES
    lru_gate_col = lru_x_col + lru_w // LANES
    xa_q_col = (2 * hg_key + 2 * hg_val + 2 * lru_w) // xa_w
    xa_gate_col = xa_q_col + 1

    w_out_b = w_out.astype(BF16)
    w_kv_b = w_kv.astype(BF16)
    wri = jnp.concatenate([w_r, w_i], axis=-1).astype(BF16)
    bri = jnp.concatenate([b_r, b_i], axis=-1).reshape(depth, 1, -1)
    post_g3 = jnp.concatenate([norm_g[1:], final_g[None]], axis=0).reshape(depth, 1, d_model)
    norm_g3 = norm_g.reshape(depth, 1, d_model)
    hg_norm_g3 = hg_norm_g.reshape(depth, 1, hg_val)
    conv_b3 = conv_b.reshape(depth, 1, lru_w)
    lam3 = lam.reshape(depth, 1, lru_w)

    x2 = x.reshape(batch * seq, d_model)
    kv = _kv(mem.reshape(batch * n_mem, d_model), mem_norm_g.reshape(depth, 1, d_model), w_kv_b)
    kv = kv.reshape(depth * batch, n_mem, 2 * xa_w)

    h = _norm(x2, norm_g3, 0, BF16)
    for l in range(depth):
        proj = _inproj(h, w_in, l)
        o_a = _hgrn(proj, lb_param, hg_norm_g3, l, batch, seq)
        o_b = _lru(proj, conv_w, conv_b3, wri, bri, lam3, l, batch, seq, lru_x_col, lru_gate_col)
        o_c = _attn(proj, kv, l, batch, seq, n_mem, xa_q_col, xa_gate_col)
        outs = _outproj(x2, o_a, o_b, o_c, w_out_b, l, post_g3, l + 1 == depth)
        if l + 1 < depth:
            x2, h = outs
    return outs[0].reshape(batch, seq, d_model)
```

```python
import functools

import numpy as np
import jax
import jax.numpy as jnp
from jax import lax
from jax.experimental import pallas as pl
from jax.experimental.pallas import tpu as pltpu

F32 = jnp.float32
BF16 = jnp.bfloat16

EPS = 1e-6
LRU_C = 8.0
CONV_WIDTH = 4

HG_HEADS = 8
HG_DK = 128
HG_DV = 128
LRU_BLOCKS = 8
LRU_BLOCK = 128
XA_HEADS = 4
XA_HEAD_DIM = 256

SUBLANES = 8
LANES = 128
VMEM_LIMIT_BYTES = 56 * 1024 * 1024

HG_CHUNK = 64
HG_UNROLL = 32
LRU_ROWS = 1024


def _half_tanh(x):
    h = 0.5 * x
    return h, jnp.tanh(h)


def _sigmoid(x):
    _, th = _half_tanh(x)
    return 0.5 + 0.5 * th


def _silu(x):
    h, th = _half_tanh(x)
    return h + h * th


def _tile(x, i):
    return x[i * SUBLANES:(i + 1) * SUBLANES]


def _bcast_row(tile, r):
    return jnp.broadcast_to(tile[r:r + 1, :], tile.shape)


def _rmsnorm_rows(x, g):
    ms = jnp.mean(x * x, axis=-1, keepdims=True)
    return x * lax.rsqrt(ms + EPS) * g


def _norm_kernel(x_ref, g_ref, o_ref):
    o_ref[...] = _rmsnorm_rows(x_ref[...], g_ref[...]).astype(o_ref.dtype)


def _norm(x2, g3, layer, dtype, tm=512):
    t, d = x2.shape
    return pl.pallas_call(
        _norm_kernel,
        grid=(t // tm,),
        in_specs=[pl.BlockSpec((tm, d), lambda i: (i, 0)),
                  pl.BlockSpec((None, 1, d), lambda i: (layer, 0, 0))],
        out_specs=pl.BlockSpec((tm, d), lambda i: (i, 0)),
        out_shape=jax.ShapeDtypeStruct((t, d), dtype),
        compiler_params=pltpu.CompilerParams(
            dimension_semantics=("parallel",),
            vmem_limit_bytes=VMEM_LIMIT_BYTES),
        name="norm",
    )(x2, g3)


def _inproj_kernel(h_ref, w_ref, o_ref):
    acc = jnp.dot(h_ref[...], w_ref[...].astype(BF16), preferred_element_type=F32)
    for c in range(o_ref.shape[0]):
        o_ref[c] = acc[:, c * LANES:(c + 1) * LANES]


def _inproj(h, w_in, layer, tm=2048, tn=512):
    t, d = h.shape
    n = w_in.shape[2]
    return pl.pallas_call(
        _inproj_kernel,
        grid=(t // tm, n // tn),
        in_specs=[
            pl.BlockSpec((tm, d), lambda i, j: (i, 0)),
            pl.BlockSpec((None, d, tn), lambda i, j: (layer, 0, j)),
        ],
        out_specs=pl.BlockSpec((tn // LANES, tm, LANES), lambda i, j: (j, i, 0)),
        out_shape=jax.ShapeDtypeStruct((n // LANES, t, LANES), F32),
        compiler_params=pltpu.CompilerParams(
            dimension_semantics=("parallel", "arbitrary"),
            vmem_limit_bytes=VMEM_LIMIT_BYTES),
        name="inproj",
    )(h, w_in)


def _hgrn_level_widths():
    w = HG_CHUNK // 2
    widths = []
    while w >= 1:
        widths.append(w)
        w //= 2
    return widths


def _hgrn_masks():
    t = np.arange(HG_CHUNK)[:, None]
    s = np.arange(HG_CHUNK)[None, :]
    masks = [((t // (2 * w)) == (s // (2 * w))) & ((t % (2 * w)) >= w) & ((s % (2 * w)) < w)
             for w in _hgrn_level_widths()]
    return np.stack(masks).astype(np.float32)


_NT = (((1,), (1,)), ((), ()))
_TN = (((0,), (0,)), ((), ()))


def _hgrn_kernel(q_ref, f_ref, v_ref, gate_ref, lbp_ref, ng_ref, masks_ref, o_ref,
                 st_ref, xs_ref, qi_ref, ks_ref, vb_ref, dec_ref, od_ref, sc_ref, op_ref, vb2_ref,
                 *, layer):
    c = HG_CHUNK
    nt = c // SUBLANES
    widths = _hgrn_level_widths()
    n_chunks = q_ref.shape[0] // c
    lbp = lbp_ref[...]
    lbe = jnp.exp(lbp - jnp.max(lbp, axis=0, keepdims=True))
    sm = lbe / jnp.sum(lbe, axis=0, keepdims=True)
    lb = jnp.sum(sm[1:layer + 1], axis=0, keepdims=True) if layer else jnp.zeros((1, HG_DK), F32)
    f_mid = 0.5 * (1.0 + lb)
    f_half = 0.5 * (1.0 - lb)
    ng = ng_ref[...]

    row8 = lax.broadcasted_iota(jnp.int32, (SUBLANES, LANES), 0)
    scan_masks = [(k, (row8 >= k).astype(F32)) for k in (1, 2, 4)]
    second = {w: (row8 & w) != 0 for w in (4, 2, 1)}
    sign = {w: jnp.where(second[w], 1.0, -1.0) for w in (4, 2)}
    low4 = row8 < 4

    def rows(idx):
        return pl.ds(pl.multiple_of(idx * c, c), c)

    def front(idx):
        sl = rows(idx)
        q = q_ref[sl, :]
        v = v_ref[sl, :]
        f = f_mid + f_half * jnp.tanh(0.5 * f_ref[sl, :])
        kk = 1.0 - f
        g = jnp.log2(f)

        cum, ends = [], []
        for i in range(nt):
            t = _tile(g, i)
            for k, m in scan_masks:
                t = t + pltpu.roll(t, k, 0) * m
            if i:
                t = t + ends[-1]
            cum.append(t)
            ends.append(_bcast_row(t, SUBLANES - 1))

        xs = []
        for w in widths:
            tiles = []
            for i in range(nt):
                qt, kt, ct = _tile(q, i), _tile(kk, i), cum[i]
                if w >= SUBLANES:
                    r = i * SUBLANES
                    mid = ends[(r - r % (2 * w) + w) // SUBLANES - 1]
                    x = qt * jnp.exp2(ct - mid) if r & w else kt * jnp.exp2(mid - ct)
                elif w == 4:
                    x = jnp.where(second[4], qt, kt) * jnp.exp2((ct - _bcast_row(ct, 3)) * sign[4])
                elif w == 2:
                    mid = jnp.where(low4, _bcast_row(ct, 1), _bcast_row(ct, 5))
                    x = jnp.where(second[2], qt, kt) * jnp.exp2((ct - mid) * sign[2])
                else:
                    x = jnp.where(second[1], qt * _tile(f, i), kt)
                tiles.append(x)
            xs.append(jnp.concatenate(tiles, axis=0).astype(BF16))

        last = ends[-1]
        qi = jnp.concatenate([_tile(q, i) * jnp.exp2(cum[i]) for i in range(nt)], axis=0)
        ks = jnp.concatenate([_tile(kk, i) * jnp.exp2(last - cum[i]) for i in range(nt)], axis=0)
        dec = jnp.exp2(last)
        od = jnp.sum(q * kk, axis=-1, keepdims=True) * v
        return xs, qi.astype(BF16), ks.astype(BF16), v.astype(BF16), dec, od

    def put_front(xs, qi, ks, vb, dec, od):
        for lvl in range(len(widths)):
            xs_ref[lvl] = xs[lvl]
        qi_ref[...] = qi
        ks_ref[...] = ks
        vb_ref[...] = vb
        dec_ref[...] = dec
        od_ref[...] = od

    def mid_start():
        a = [lax.dot_general(xs_ref[lvl], xs_ref[lvl], _NT, preferred_element_type=F32)
             for lvl in range(len(widths))]
        vb = vb_ref[...]
        upd = lax.dot_general(vb, ks_ref[...], _TN, preferred_element_type=F32)
        st = st_ref[...]
        o_inter = lax.dot_general(qi_ref[...], st.astype(BF16), _NT, preferred_element_type=F32)
        return st, vb, o_inter, upd, a

    def mid_finish(st, vb, o_inter, upd, a):
        tiles = []
        for i in range(nt):
            r = i * SUBLANES
            acc = None
            for lvl, w in enumerate(widths):
                if w >= SUBLANES and not r & w:
                    continue
                term = _tile(a[lvl], i) * masks_ref[lvl, r:r + SUBLANES, :]
                acc = term if acc is None else acc + term
            tiles.append(acc)
        sc_ref[...] = jnp.concatenate(tiles, axis=0).astype(BF16)
        op_ref[...] = o_inter + od_ref[...]
        vb2_ref[...] = vb
        st_ref[...] = st * dec_ref[0:1, :] + upd

    def back_start():
        return op_ref[...] + jnp.dot(sc_ref[...], vb2_ref[...], preferred_element_type=F32)

    def back_finish(o, idx):
        ms = jnp.mean(o * o, axis=-1, keepdims=True)
        y = o * lax.rsqrt(ms + EPS) * ng * _silu(gate_ref[rows(idx), :])
        o_ref[rows(idx), :] = y.astype(o_ref.dtype)

    st_ref[...] = jnp.zeros_like(st_ref)
    sc_ref[...] = jnp.zeros_like(sc_ref)
    op_ref[...] = jnp.zeros_like(op_ref)
    vb2_ref[...] = jnp.zeros_like(vb2_ref)
    put_front(*front(0))

    def body(i, carry):
        mid = mid_start()
        o_back = back_start()
        nxt = front(jnp.minimum(i + 1, n_chunks - 1))
        back_finish(o_back, jnp.maximum(i - 1, 0))
        mid_finish(*mid)
        put_front(*nxt)
        return carry

    lax.fori_loop(0, n_chunks, body, 0, unroll=HG_UNROLL)
    back_finish(back_start(), n_chunks - 1)


def _hgrn(proj, lb_param, ng, layer, batch, seq):
    masks = _hgrn_masks()
    c = HG_CHUNK
    t = proj.shape[1]
    nh = HG_HEADS
    depth = lb_param.shape[0]

    def col(base):
        return pl.BlockSpec((None, seq, LANES), lambda b, h: (base + h, b, 0))

    return pl.pallas_call(
        functools.partial(_hgrn_kernel, layer=layer),
        grid=(batch, nh),
        in_specs=[
            col(0), col(nh), col(2 * nh), col(3 * nh),
            pl.BlockSpec((depth, LANES), lambda b, h: (0, h)),
            pl.BlockSpec((None, 1, LANES), lambda b, h: (layer, 0, h)),
            pl.BlockSpec(masks.shape, lambda b, h: (0, 0, 0)),
        ],
        out_specs=pl.BlockSpec((seq, LANES), lambda b, h: (b, h)),
        out_shape=jax.ShapeDtypeStruct((t, nh * HG_DV), BF16),
        scratch_shapes=[
            pltpu.VMEM((HG_DV, HG_DK), F32),
            pltpu.VMEM((masks.shape[0], c, HG_DK), BF16),
            pltpu.VMEM((c, HG_DK), BF16),
            pltpu.VMEM((c, HG_DK), BF16),
            pltpu.VMEM((c, HG_DV), BF16),
            pltpu.VMEM((SUBLANES, HG_DK), F32),
            pltpu.VMEM((c, HG_DV), F32),
            pltpu.VMEM((c, c), BF16),
            pltpu.VMEM((c, HG_DV), F32),
            pltpu.VMEM((c, HG_DV), BF16),
        ],
        compiler_params=pltpu.CompilerParams(
            dimension_semantics=("parallel", "parallel"),
            vmem_limit_bytes=VMEM_LIMIT_BYTES),
        name="hgrn2",
    )(proj, proj, proj, proj, lb_param, ng, jnp.asarray(masks))


def _lru_kernel(x_ref, gate_ref, cw_ref, cb_ref, wri_ref, bri_ref, lam_ref, o_ref):
    rws = LRU_ROWS
    nt = rws // SUBLANES
    n_steps = x_ref.shape[0] // rws
    cw = cw_ref[...]
    cb = cb_ref[...]
    bri = bri_ref[...]
    lam = lam_ref[...]
    log_a_scale = (-LRU_C) * (jnp.log1p(jnp.exp(-jnp.abs(lam))) + jnp.maximum(-lam, 0.0))
    tap = [cw[CONV_WIDTH - 1 - s:CONV_WIDTH - s, :] for s in range(CONV_WIDTH)]
    row8 = lax.broadcasted_iota(jnp.int32, (SUBLANES, LANES), 0)
    from_prev = [row8 < s for s in range(CONV_WIDTH)]
    scan_masks = [(k, row8 >= k) for k in (1, 2, 4)]

    def step(n, carry):
        prev8, hlast = carry
        r0 = pl.multiple_of(n * rws, rws)
        x = x_ref[pl.ds(r0, rws), :]

        prev_rolled = [pltpu.roll(prev8, s, 0) for s in range(1, CONV_WIDTH)]
        xc_tiles = []
        for j in range(nt):
            t = _tile(x, j)
            rolled = [pltpu.roll(t, s, 0) for s in range(1, CONV_WIDTH)]
            acc = t * tap[0] + cb
            for s in range(1, CONV_WIDTH):
                acc = acc + jnp.where(from_prev[s], prev_rolled[s - 1], rolled[s - 1]) * tap[s]
            prev_rolled = rolled
            xc_tiles.append(acc)
        xc = jnp.concatenate(xc_tiles, axis=0)

        ri = jnp.dot(xc.astype(BF16), wri_ref[...], preferred_element_type=F32) + bri
        log_a = _sigmoid(ri[:, :LANES]) * log_a_scale
        a = jnp.exp(log_a)
        u = jnp.sqrt(-jnp.tanh(log_a) * (a * a + 1.0)) * (_sigmoid(ri[:, LANES:]) * xc)

        hs = []
        for j in range(nt):
            at, ut = _tile(a, j), _tile(u, j)
            for k, m in scan_masks:
                ut = ut + at * jnp.where(m, pltpu.roll(ut, k, 0), 0.0)
                at = at * jnp.where(m, pltpu.roll(at, k, 0), 1.0)
            h8 = at * hlast + ut
            hlast = _bcast_row(h8, SUBLANES - 1)
            hs.append(h8)
        h = jnp.concatenate(hs, axis=0)

        y = h * _silu(gate_ref[pl.ds(r0, rws), :])
        o_ref[pl.ds(r0, rws), :] = y.astype(o_ref.dtype)
        return _tile(x, nt - 1), hlast

    zeros8 = jnp.zeros((SUBLANES, LANES), F32)
    lax.fori_loop(0, n_steps, step, (zeros8, zeros8))


def _lru(proj, cw, cb, wri, bri, lam, layer, batch, seq, x_col, gate_col):
    t = proj.shape[1]
    nb = LRU_BLOCKS

    def col(base):
        return pl.BlockSpec((None, seq, LANES), lambda b, n: (base + n, b, 0))

    def vec(rows, width=LANES):
        return pl.BlockSpec((None, rows, width), lambda b, n: (layer, 0, n))

    return pl.pallas_call(
        _lru_kernel,
        grid=(batch, nb),
        in_specs=[
            col(x_col), col(gate_col),
            vec(CONV_WIDTH), vec(1),
            pl.BlockSpec((None, None, LRU_BLOCK, 2 * LRU_BLOCK), lambda b, n: (layer, n, 0, 0)),
            vec(1, 2 * LANES),
            vec(1),
        ],
        out_specs=pl.BlockSpec((seq, LANES), lambda b, n: (b, n)),
        out_shape=jax.ShapeDtypeStruct((t, nb * LRU_BLOCK), BF16),
        compiler_params=pltpu.CompilerParams(
            dimension_semantics=("parallel", "parallel"),
            vmem_limit_bytes=VMEM_LIMIT_BYTES),
        name="rglru",
    )(proj, proj, cw, cb, wri, bri, lam)


def _kv_kernel(mem_ref, g_ref, w_ref, o_ref, h_ref):
    @pl.when(pl.program_id(1) == 0)
    def _():
        h_ref[...] = _rmsnorm_rows(mem_ref[...], g_ref[...]).astype(BF16)

    o_ref[...] = jnp.dot(h_ref[...], w_ref[...].astype(BF16),
                         preferred_element_type=F32).astype(o_ref.dtype)


def _kv(mem2, g, w_kv, tn=512):
    depth, d, n = w_kv.shape
    m = mem2.shape[0]
    return pl.pallas_call(
        _kv_kernel,
        grid=(depth, n // tn),
        in_specs=[
            pl.BlockSpec((m, d), lambda l, j: (0, 0)),
            pl.BlockSpec((None, 1, d), lambda l, j: (l, 0, 0)),
            pl.BlockSpec((None, d, tn), lambda l, j: (l, 0, j)),
        ],
        out_specs=pl.BlockSpec((None, m, tn), lambda l, j: (l, 0, j)),
        out_shape=jax.ShapeDtypeStruct((depth, m, n), BF16),
        scratch_shapes=[pltpu.VMEM((m, d), BF16)],
        compiler_params=pltpu.CompilerParams(
            dimension_semantics=("parallel", "arbitrary"),
            vmem_limit_bytes=VMEM_LIMIT_BYTES),
        name="mem_kv",
    )(mem2, g, w_kv)


def _attn_kernel(q_ref, gate_ref, k_ref, v_ref, o_ref):
    scale = XA_HEAD_DIM ** -0.5
    per_head = XA_HEAD_DIM // LANES

    def head_cols(ref, h):
        return jnp.concatenate([ref[h * per_head + c] for c in range(per_head)], axis=1)

    for h in range(XA_HEADS):
        sl = slice(h * XA_HEAD_DIM, (h + 1) * XA_HEAD_DIM)
        s = lax.dot_general(head_cols(q_ref, h).astype(BF16), k_ref[:, sl], _NT,
                            preferred_element_type=F32) * scale
        m = jnp.max(s, axis=-1, keepdims=True)
        p = jnp.exp(s - m)
        p = p / jnp.sum(p, axis=-1, keepdims=True)
        o = jnp.dot(p.astype(BF16), v_ref[:, sl], preferred_element_type=F32)
        o_ref[:, sl] = (o * _silu(head_cols(gate_ref, h))).astype(o_ref.dtype)


def _attn(proj, kv, layer, batch, seq, n_mem, q_col, gate_col, ts=1024):
    t = proj.shape[1]
    width = XA_HEADS * XA_HEAD_DIM
    nt = seq // ts
    cols = width // LANES
    return pl.pallas_call(
        _attn_kernel,
        grid=(batch, nt),
        in_specs=[
            pl.BlockSpec((cols, ts, LANES), lambda b, i: (q_col, b * nt + i, 0)),
            pl.BlockSpec((cols, ts, LANES), lambda b, i: (gate_col, b * nt + i, 0)),
            pl.BlockSpec((None, n_mem, width), lambda b, i: (layer * batch + b, 0, 0)),
            pl.BlockSpec((None, n_mem, width), lambda b, i: (layer * batch + b, 0, 1)),
        ],
        out_specs=pl.BlockSpec((ts, width), lambda b, i: (b * nt + i, 0)),
        out_shape=jax.ShapeDtypeStruct((t, width), BF16),
        compiler_params=pltpu.CompilerParams(
            dimension_semantics=("parallel", "parallel"),
            vmem_limit_bytes=VMEM_LIMIT_BYTES),
        name="mem_attn",
    )(proj, proj, kv, kv)


def _outproj_kernel(x_ref, a_ref, b_ref, c_ref, w_ref, g_ref, *out_refs):
    mixed = jnp.concatenate([a_ref[...], b_ref[...], c_ref[...]], axis=1)
    acc = x_ref[...] + jnp.dot(mixed, w_ref[...], preferred_element_type=F32)
    y_ref = out_refs[-1]
    y_ref[...] = _rmsnorm_rows(acc, g_ref[...]).astype(y_ref.dtype)
    if len(out_refs) == 2:
        out_refs[0][...] = acc


def _outproj(x2, oa, ob, oc, w_bf16, layer, g3, last, tm=512):
    t, d = x2.shape
    k = w_bf16.shape[1]

    def lhs(arr):
        return pl.BlockSpec((tm, arr.shape[1]), lambda i: (i, 0))

    row_block = pl.BlockSpec((tm, d), lambda i: (i, 0))
    if last:
        out_specs = [row_block]
        out_shape = [jax.ShapeDtypeStruct((t, d), F32)]
    else:
        out_specs = [row_block, row_block]
        out_shape = [jax.ShapeDtypeStruct((t, d), F32), jax.ShapeDtypeStruct((t, d), BF16)]
    return pl.pallas_call(
        _outproj_kernel,
        grid=(t // tm,),
        in_specs=[
            row_block,
            lhs(oa), lhs(ob), lhs(oc),
            pl.BlockSpec((None, k, d), lambda i: (layer, 0, 0), pipeline_mode=pl.Buffered(1)),
            pl.BlockSpec((None, 1, d), lambda i: (layer, 0, 0)),
        ],
        out_specs=out_specs,
        out_shape=out_shape,
        compiler_params=pltpu.CompilerParams(
            dimension_semantics=("parallel",),
            vmem_limit_bytes=VMEM_LIMIT_BYTES),
        name="outproj",
    )(x2, oa, ob, oc, w_bf16, g3)


def kernel(x, mem, norm_g, w_in, lb_param, hg_norm_g, conv_w, conv_b, w_r, b_r, w_i, b_i, lam,
           mem_norm_g, w_kv, w_out, final_g):
    batch, seq, d_model = x.shape
    depth = w_in.shape[0]
    n_mem = mem.shape[1]
    hg_key = HG_HEADS * HG_DK
    hg_val = HG_HEADS * HG_DV
    lru_w = LRU_BLOCKS * LRU_BLOCK
    xa_w = XA_HEADS * XA_HEAD_DIM
    assert seq % (HG_CHUNK * HG_UNROLL) == 0 and seq % LRU_ROWS == 0
    assert w_in.shape[2] == 2 * hg_key + 2 * hg_val + 2 * lru_w + 2 * xa_w
    assert hg_key == hg_val == lru_w == xa_w
    lru_x_col = (2 * hg_key + 2 * hg_val) // LANES
    lru_gate_col = lru_x_col + lru_w // LANES
    xa_q_col = (2 * hg_key + 2 * hg_val + 2 * lru_w) // xa_w
    xa_gate_col = xa_q_col + 1

    w_out_b = w_out.astype(BF16)
    wri = jnp.concatenate([w_r, w_i], axis=-1).astype(BF16)
    bri = jnp.concatenate([b_r, b_i], axis=-1).reshape(depth, 1, -1)
    post_g3 = jnp.concatenate([norm_g[1:], final_g[None]], axis=0).reshape(depth, 1, d_model)
    norm_g3 = norm_g.reshape(depth, 1, d_model)
    hg_norm_g3 = hg_norm_g.reshape(depth, 1, hg_val)
    conv_b3 = conv_b.reshape(depth, 1, lru_w)
    lam3 = lam.reshape(depth, 1, lru_w)

    x2 = x.reshape(batch * seq, d_model)
    kv = _kv(mem.reshape(batch * n_mem, d_model), mem_norm_g.reshape(depth, 1, d_model), w_kv)
    kv = kv.reshape(depth * batch, n_mem, 2 * xa_w)

    h = _norm(x2, norm_g3, 0, BF16)
    for l in range(depth):
        proj = _inproj(h, w_in, l)
        o_a = _hgrn(proj, lb_param, hg_norm_g3, l, batch, seq)
        o_b = _lru(proj, conv_w, conv_b3, wri, bri, lam3, l, batch, seq, lru_x_col, lru_gate_col)
        o_c = _attn(proj, kv, l, batch, seq, n_mem, xa_q_col, xa_gate_col)
        outs = _outproj(x2, o_a, o_b, o_c, w_out_b, l, post_g3, l + 1 == depth)
        if l + 1 < depth:
            x2, h = outs
    return outs[0].reshape(batch, seq, d_model)
```

```python
import functools

import numpy as np
import jax
import jax.numpy as jnp
from jax import lax
from jax.experimental import pallas as pl
from jax.experimental.pallas import tpu as pltpu

F32 = jnp.float32
BF16 = jnp.bfloat16

EPS = 1e-6
LRU_C = 8.0
CONV_WIDTH = 4

HG_HEADS = 8
HG_DK = 128
HG_DV = 128
LRU_BLOCKS = 8
LRU_BLOCK = 128
XA_HEADS = 4
XA_HEAD_DIM = 256

SUBLANES = 8
LANES = 128
VMEM_LIMIT_BYTES = 56 * 1024 * 1024

HG_CHUNK = 64
LRU_ROWS = 1024


def _half_tanh(x):
    h = 0.5 * x
    return h, jnp.tanh(h)


def _sigmoid(x):
    _, th = _half_tanh(x)
    return 0.5 + 0.5 * th


def _silu(x):
    h, th = _half_tanh(x)
    return h + h * th


def _tile(x, i):
    return x[i * SUBLANES:(i + 1) * SUBLANES]


def _bcast_row(tile, r):
    return jnp.broadcast_to(tile[r:r + 1, :], tile.shape)


def _rmsnorm_rows(x, g):
    ms = jnp.mean(x * x, axis=-1, keepdims=True)
    return x * lax.rsqrt(ms + EPS) * g


def _norm_kernel(x_ref, g_ref, o_ref):
    o_ref[...] = _rmsnorm_rows(x_ref[...], g_ref[...]).astype(o_ref.dtype)


def _norm(x2, g3, layer, dtype, tm=512):
    t, d = x2.shape
    return pl.pallas_call(
        _norm_kernel,
        grid=(t // tm,),
        in_specs=[pl.BlockSpec((tm, d), lambda i: (i, 0)),
                  pl.BlockSpec((None, 1, d), lambda i: (layer, 0, 0))],
        out_specs=pl.BlockSpec((tm, d), lambda i: (i, 0)),
        out_shape=jax.ShapeDtypeStruct((t, d), dtype),
        compiler_params=pltpu.CompilerParams(
            dimension_semantics=("parallel",),
            vmem_limit_bytes=VMEM_LIMIT_BYTES),
        name="norm",
    )(x2, g3)


def _inproj_kernel(h_ref, w_ref, o_ref):
    acc = jnp.dot(h_ref[...], w_ref[...].astype(BF16), preferred_element_type=F32)
    for c in range(o_ref.shape[0]):
        o_ref[c] = acc[:, c * LANES:(c + 1) * LANES]


def _inproj(h, w_in, layer, tm=2048, tn=512):
    t, d = h.shape
    n = w_in.shape[2]
    return pl.pallas_call(
        _inproj_kernel,
        grid=(t // tm, n // tn),
        in_specs=[
            pl.BlockSpec((tm, d), lambda i, j: (i, 0)),
            pl.BlockSpec((None, d, tn), lambda i, j: (layer, 0, j)),
        ],
        out_specs=pl.BlockSpec((tn // LANES, tm, LANES), lambda i, j: (j, i, 0)),
        out_shape=jax.ShapeDtypeStruct((n // LANES, t, LANES), F32),
        compiler_params=pltpu.CompilerParams(
            dimension_semantics=("parallel", "arbitrary"),
            vmem_limit_bytes=VMEM_LIMIT_BYTES),
        name="inproj",
    )(h, w_in)


def _hgrn_level_widths():
    w = HG_CHUNK // 2
    widths = []
    while w >= 1:
        widths.append(w)
        w //= 2
    return widths


def _hgrn_masks():
    t = np.arange(HG_CHUNK)[:, None]
    s = np.arange(HG_CHUNK)[None, :]
    masks = [((t // (2 * w)) == (s // (2 * w))) & ((t % (2 * w)) >= w) & ((s % (2 * w)) < w)
             for w in _hgrn_level_widths()]
    return np.stack(masks).astype(np.float32)


_NT = (((1,), (1,)), ((), ()))
_TN = (((0,), (0,)), ((), ()))


def _hgrn_kernel(q_ref, f_ref, v_ref, gate_ref, lbp_ref, ng_ref, masks_ref, o_ref, *, layer):
    c = HG_CHUNK
    nt = c // SUBLANES
    widths = _hgrn_level_widths()
    n_chunks = q_ref.shape[0] // c
    lbp = lbp_ref[...]
    lbe = jnp.exp(lbp - jnp.max(lbp, axis=0, keepdims=True))
    sm = lbe / jnp.sum(lbe, axis=0, keepdims=True)
    lb = jnp.sum(sm[1:layer + 1], axis=0, keepdims=True) if layer else jnp.zeros((1, HG_DK), F32)
    f_mid = 0.5 * (1.0 + lb)
    f_half = 0.5 * (1.0 - lb)
    ng = ng_ref[...]

    row8 = lax.broadcasted_iota(jnp.int32, (SUBLANES, LANES), 0)
    scan_masks = [(k, (row8 >= k).astype(F32)) for k in (1, 2, 4)]
    second = {w: (row8 & w) != 0 for w in (4, 2, 1)}
    sign = {w: jnp.where(second[w], 1.0, -1.0) for w in (4, 2)}
    low4 = row8 < 4

    def rows(idx):
        return slice(idx * c, (idx + 1) * c)

    def front(idx):
        sl = rows(idx)
        q = q_ref[sl, :]
        v = v_ref[sl, :]
        f = f_mid + f_half * jnp.tanh(0.5 * f_ref[sl, :])
        kk = 1.0 - f
        g = jnp.log2(f)

        cum, ends = [], []
        for i in range(nt):
            t = _tile(g, i)
            for k, m in scan_masks:
                t = t + pltpu.roll(t, k, 0) * m
            if i:
                t = t + ends[-1]
            cum.append(t)
            ends.append(_bcast_row(t, SUBLANES - 1))

        xs = []
        for w in widths:
            tiles = []
            for i in range(nt):
                qt, kt, ct = _tile(q, i), _tile(kk, i), cum[i]
                if w >= SUBLANES:
                    r = i * SUBLANES
                    mid = ends[(r - r % (2 * w) + w) // SUBLANES - 1]
                    x = qt * jnp.exp2(ct - mid) if r & w else kt * jnp.exp2(mid - ct)
                elif w == 4:
                    x = jnp.where(second[4], qt, kt) * jnp.exp2((ct - _bcast_row(ct, 3)) * sign[4])
                elif w == 2:
                    mid = jnp.where(low4, _bcast_row(ct, 1), _bcast_row(ct, 5))
                    x = jnp.where(second[2], qt, kt) * jnp.exp2((ct - mid) * sign[2])
                else:
                    x = jnp.where(second[1], qt * _tile(f, i), kt)
                tiles.append(x)
            xs.append(jnp.concatenate(tiles, axis=0).astype(BF16))

        last = ends[-1]
        qi = jnp.concatenate([_tile(q, i) * jnp.exp2(cum[i]) for i in range(nt)], axis=0)
        ks = jnp.concatenate([_tile(kk, i) * jnp.exp2(last - cum[i]) for i in range(nt)], axis=0)
        dec = jnp.exp2(last)
        od = jnp.sum(q * kk, axis=-1, keepdims=True) * v
        return xs, qi.astype(BF16), ks.astype(BF16), v.astype(BF16), dec, od

    def mid_start(fr, st):
        xs, qi, ks, vb, dec, od = fr
        a = [lax.dot_general(x, x, _NT, preferred_element_type=F32) for x in xs]
        upd = lax.dot_general(vb, ks, _TN, preferred_element_type=F32)
        o_inter = lax.dot_general(qi, st.astype(BF16), _NT, preferred_element_type=F32)
        return a, upd, o_inter

    def mid_finish(fr, st, a, upd, o_inter):
        _, _, _, vb, dec, od = fr
        tiles = []
        for i in range(nt):
            r = i * SUBLANES
            acc = None
            for lvl, w in enumerate(widths):
                if w >= SUBLANES and not r & w:
                    continue
                term = _tile(a[lvl], i) * masks_ref[lvl, r:r + SUBLANES, :]
                acc = term if acc is None else acc + term
            tiles.append(acc)
        scores = jnp.concatenate(tiles, axis=0).astype(BF16)
        return (scores, o_inter + od, vb), st * dec[0:1, :] + upd

    def back_start(md):
        scores, o_partial, vb = md
        return o_partial + jnp.dot(scores, vb, preferred_element_type=F32)

    def back_finish(o, idx):
        ms = jnp.mean(o * o, axis=-1, keepdims=True)
        y = o * lax.rsqrt(ms + EPS) * ng * _silu(gate_ref[rows(idx), :])
        o_ref[rows(idx), :] = y.astype(o_ref.dtype)

    st = jnp.zeros((HG_DV, HG_DK), F32)
    fr = front(0)
    md = None
    for i in range(n_chunks):
        started = mid_start(fr, st)
        o_back = back_start(md) if md is not None else None
        nxt = front(i + 1) if i + 1 < n_chunks else None
        if o_back is not None:
            back_finish(o_back, i - 1)
        md, st = mid_finish(fr, st, *started)
        fr = nxt
    back_finish(back_start(md), n_chunks - 1)


def _hgrn(proj, lb_param, ng, layer, batch, seq):
    masks = _hgrn_masks()
    t = proj.shape[1]
    nh = HG_HEADS
    depth = lb_param.shape[0]

    def col(base):
        return pl.BlockSpec((None, seq, LANES), lambda b, h: (base + h, b, 0))

    return pl.pallas_call(
        functools.partial(_hgrn_kernel, layer=layer),
        grid=(batch, nh),
        in_specs=[
            col(0), col(nh), col(2 * nh), col(3 * nh),
            pl.BlockSpec((depth, LANES), lambda b, h: (0, h)),
            pl.BlockSpec((None, 1, LANES), lambda b, h: (layer, 0, h)),
            pl.BlockSpec(masks.shape, lambda b, h: (0, 0, 0)),
        ],
        out_specs=pl.BlockSpec((seq, LANES), lambda b, h: (b, h)),
        out_shape=jax.ShapeDtypeStruct((t, nh * HG_DV), BF16),
        compiler_params=pltpu.CompilerParams(
            dimension_semantics=("parallel", "parallel"),
            vmem_limit_bytes=VMEM_LIMIT_BYTES),
        name="hgrn2",
    )(proj, proj, proj, proj, lb_param, ng, jnp.asarray(masks))


def _lru_kernel(x_ref, gate_ref, cw_ref, cb_ref, wri_ref, bri_ref, lam_ref, wo_ref, o_ref, wob_ref):
    wob_ref[...] = wo_ref[...].astype(BF16)
    rws = LRU_ROWS
    nt = rws // SUBLANES
    n_steps = x_ref.shape[0] // rws
    cw = cw_ref[...]
    cb = cb_ref[...]
    bri = bri_ref[...]
    lam = lam_ref[...]
    log_a_scale = (-LRU_C) * (jnp.log1p(jnp.exp(-jnp.abs(lam))) + jnp.maximum(-lam, 0.0))
    tap = [cw[CONV_WIDTH - 1 - s:CONV_WIDTH - s, :] for s in range(CONV_WIDTH)]
    row8 = lax.broadcasted_iota(jnp.int32, (SUBLANES, LANES), 0)
    from_prev = [row8 < s for s in range(CONV_WIDTH)]
    scan_masks = [(k, row8 >= k) for k in (1, 2, 4)]

    def step(n, carry):
        prev8, hlast = carry
        r0 = pl.multiple_of(n * rws, rws)
        x = x_ref[pl.ds(r0, rws), :]

        prev_rolled = [pltpu.roll(prev8, s, 0) for s in range(1, CONV_WIDTH)]
        xc_tiles = []
        for j in range(nt):
            t = _tile(x, j)
            rolled = [pltpu.roll(t, s, 0) for s in range(1, CONV_WIDTH)]
            acc = t * tap[0] + cb
            for s in range(1, CONV_WIDTH):
                acc = acc + jnp.where(from_prev[s], prev_rolled[s - 1], rolled[s - 1]) * tap[s]
            prev_rolled = rolled
            xc_tiles.append(acc)
        xc = jnp.concatenate(xc_tiles, axis=0)

        ri = jnp.dot(xc.astype(BF16), wri_ref[...], preferred_element_type=F32) + bri
        log_a = _sigmoid(ri[:, :LANES]) * log_a_scale
        a = jnp.exp(log_a)
        u = jnp.sqrt(-jnp.tanh(log_a) * (a * a + 1.0)) * (_sigmoid(ri[:, LANES:]) * xc)

        hs = []
        for j in range(nt):
            at, ut = _tile(a, j), _tile(u, j)
            for k, m in scan_masks:
                ut = ut + at * jnp.where(m, pltpu.roll(ut, k, 0), 0.0)
                at = at * jnp.where(m, pltpu.roll(at, k, 0), 1.0)
            h8 = at * hlast + ut
            hlast = _bcast_row(h8, SUBLANES - 1)
            hs.append(h8)
        h = jnp.concatenate(hs, axis=0)

        y = h * _silu(gate_ref[pl.ds(r0, rws), :])
        o_ref[pl.ds(r0, rws), :] = y.astype(o_ref.dtype)
        return _tile(x, nt - 1), hlast

    zeros8 = jnp.zeros((SUBLANES, LANES), F32)
    lax.fori_loop(0, n_steps, step, (zeros8, zeros8))


def _lru(proj, cw, cb, wri, bri, lam, w_out, layer, batch, seq, x_col, gate_col):
    t = proj.shape[1]
    nb = LRU_BLOCKS
    k, d = w_out.shape[1:]
    wo_rows = k // (batch * nb)
    wo_spec = pl.BlockSpec((None, wo_rows, d), lambda b, n: (layer, b * nb + n, 0))
    wob_spec = pl.BlockSpec((wo_rows, d), lambda b, n: (b * nb + n, 0))

    def col(base):
        return pl.BlockSpec((None, seq, LANES), lambda b, n: (base + n, b, 0))

    def vec(rows, width=LANES):
        return pl.BlockSpec((None, rows, width), lambda b, n: (layer, 0, n))

    return pl.pallas_call(
        _lru_kernel,
        grid=(batch, nb),
        in_specs=[
            col(x_col), col(gate_col),
            vec(CONV_WIDTH), vec(1),
            pl.BlockSpec((None, None, LRU_BLOCK, 2 * LRU_BLOCK), lambda b, n: (layer, n, 0, 0)),
            vec(1, 2 * LANES),
            vec(1),
            wo_spec,
        ],
        out_specs=[pl.BlockSpec((seq, LANES), lambda b, n: (b, n)), wob_spec],
        out_shape=[jax.ShapeDtypeStruct((t, nb * LRU_BLOCK), BF16),
                   jax.ShapeDtypeStruct((k, d), BF16)],
        compiler_params=pltpu.CompilerParams(
            dimension_semantics=("parallel", "parallel"),
            vmem_limit_bytes=VMEM_LIMIT_BYTES),
        name="rglru",
    )(proj, proj, cw, cb, wri, bri, lam, w_out)


def _kv_kernel(mem_ref, g_ref, w_ref, o_ref, h_ref):
    @pl.when(pl.program_id(1) == 0)
    def _():
        h_ref[...] = _rmsnorm_rows(mem_ref[...], g_ref[...]).astype(BF16)

    o_ref[...] = jnp.dot(h_ref[...], w_ref[...].astype(BF16),
                         preferred_element_type=F32).astype(o_ref.dtype)


def _kv(mem2, g, w_kv, tn=512):
    depth, d, n = w_kv.shape
    m = mem2.shape[0]
    return pl.pallas_call(
        _kv_kernel,
        grid=(depth, n // tn),
        in_specs=[
            pl.BlockSpec((m, d), lambda l, j: (0, 0)),
            pl.BlockSpec((None, 1, d), lambda l, j: (l, 0, 0)),
            pl.BlockSpec((None, d, tn), lambda l, j: (l, 0, j)),
        ],
        out_specs=pl.BlockSpec((None, m, tn), lambda l, j: (l, 0, j)),
        out_shape=jax.ShapeDtypeStruct((depth, m, n), BF16),
        scratch_shapes=[pltpu.VMEM((m, d), BF16)],
        compiler_params=pltpu.CompilerParams(
            dimension_semantics=("parallel", "arbitrary"),
            vmem_limit_bytes=VMEM_LIMIT_BYTES),
        name="mem_kv",
    )(mem2, g, w_kv)


def _attn_kernel(q_ref, gate_ref, k_ref, v_ref, o_ref):
    scale = XA_HEAD_DIM ** -0.5
    per_head = XA_HEAD_DIM // LANES

    def head_cols(ref, h):
        return jnp.concatenate([ref[h * per_head + c] for c in range(per_head)], axis=1)

    for h in range(XA_HEADS):
        sl = slice(h * XA_HEAD_DIM, (h + 1) * XA_HEAD_DIM)
        s = lax.dot_general(head_cols(q_ref, h).astype(BF16), k_ref[:, sl], _NT,
                            preferred_element_type=F32) * scale
        m = jnp.max(s, axis=-1, keepdims=True)
        p = jnp.exp(s - m)
        p = p / jnp.sum(p, axis=-1, keepdims=True)
        o = jnp.dot(p.astype(BF16), v_ref[:, sl], preferred_element_type=F32)
        o_ref[:, sl] = (o * _silu(head_cols(gate_ref, h))).astype(o_ref.dtype)


def _attn(proj, kv, layer, batch, seq, n_mem, q_col, gate_col, ts=1024):
    t = proj.shape[1]
    width = XA_HEADS * XA_HEAD_DIM
    nt = seq // ts
    cols = width // LANES
    return pl.pallas_call(
        _attn_kernel,
        grid=(batch, nt),
        in_specs=[
            pl.BlockSpec((cols, ts, LANES), lambda b, i: (q_col, b * nt + i, 0)),
            pl.BlockSpec((cols, ts, LANES), lambda b, i: (gate_col, b * nt + i, 0)),
            pl.BlockSpec((None, n_mem, width), lambda b, i: (layer * batch + b, 0, 0)),
            pl.BlockSpec((None, n_mem, width), lambda b, i: (layer * batch + b, 0, 1)),
        ],
        out_specs=pl.BlockSpec((ts, width), lambda b, i: (b * nt + i, 0)),
        out_shape=jax.ShapeDtypeStruct((t, width), BF16),
        compiler_params=pltpu.CompilerParams(
            dimension_semantics=("parallel", "parallel"),
            vmem_limit_bytes=VMEM_LIMIT_BYTES),
        name="mem_attn",
    )(proj, proj, kv, kv)


def _outproj_kernel(x_ref, a_ref, b_ref, c_ref, w_ref, g_ref, *out_refs):
    mixed = jnp.concatenate([a_ref[...], b_ref[...], c_ref[...]], axis=1)
    acc = x_ref[...] + jnp.dot(mixed, w_ref[...], preferred_element_type=F32)
    y_ref = out_refs[-1]
    y_ref[...] = _rmsnorm_rows(acc, g_ref[...]).astype(y_ref.dtype)
    if len(out_refs) == 2:
        out_refs[0][...] = acc


def _outproj(x2, oa, ob, oc, w_bf16, layer, g3, last, tm=512):
    t, d = x2.shape
    k = w_bf16.shape[0]

    def lhs(arr):
        return pl.BlockSpec((tm, arr.shape[1]), lambda i: (i, 0))

    row_block = pl.BlockSpec((tm, d), lambda i: (i, 0))
    if last:
        out_specs = [row_block]
        out_shape = [jax.ShapeDtypeStruct((t, d), F32)]
    else:
        out_specs = [row_block, row_block]
        out_shape = [jax.ShapeDtypeStruct((t, d), F32), jax.ShapeDtypeStruct((t, d), BF16)]
    return pl.pallas_call(
        _outproj_kernel,
        grid=(t // tm,),
        in_specs=[
            row_block,
            lhs(oa), lhs(ob), lhs(oc),
            pl.BlockSpec((k, d), lambda i: (0, 0), pipeline_mode=pl.Buffered(1)),
            pl.BlockSpec((None, 1, d), lambda i: (layer, 0, 0)),
        ],
        out_specs=out_specs,
        out_shape=out_shape,
        compiler_params=pltpu.CompilerParams(
            dimension_semantics=("parallel",),
            vmem_limit_bytes=VMEM_LIMIT_BYTES),
        name="outproj",
    )(x2, oa, ob, oc, w_bf16, g3)


def kernel(x, mem, norm_g, w_in, lb_param, hg_norm_g, conv_w, conv_b, w_r, b_r, w_i, b_i, lam,
           mem_norm_g, w_kv, w_out, final_g):
    batch, seq, d_model = x.shape
    depth = w_in.shape[0]
    n_mem = mem.shape[1]
    hg_key = HG_HEADS * HG_DK
    hg_val = HG_HEADS * HG_DV
    lru_w = LRU_BLOCKS * LRU_BLOCK
    xa_w = XA_HEADS * XA_HEAD_DIM
    assert seq % HG_CHUNK == 0 and seq % LRU_ROWS == 0
    assert w_in.shape[2] == 2 * hg_key + 2 * hg_val + 2 * lru_w + 2 * xa_w
    assert hg_key == hg_val == lru_w == xa_w
    lru_x_col = (2 * hg_key + 2 * hg_val) // LANES
    lru_gate_col = lru_x_col + lru_w // LANES
    xa_q_col = (2 * hg_key + 2 * hg_val + 2 * lru_w) // xa_w
    xa_gate_col = xa_q_col + 1

    wri = jnp.concatenate([w_r, w_i], axis=-1).astype(BF16)
    bri = jnp.concatenate([b_r, b_i], axis=-1).reshape(depth, 1, -1)
    post_g3 = jnp.concatenate([norm_g[1:], final_g[None]], axis=0).reshape(depth, 1, d_model)
    norm_g3 = norm_g.reshape(depth, 1, d_model)
    hg_norm_g3 = hg_norm_g.reshape(depth, 1, hg_val)
    conv_b3 = conv_b.reshape(depth, 1, lru_w)
    lam3 = lam.reshape(depth, 1, lru_w)

    x2 = x.reshape(batch * seq, d_model)
    kv = _kv(mem.reshape(batch * n_mem, d_model), mem_norm_g.reshape(depth, 1, d_model), w_kv)
    kv = kv.reshape(depth * batch, n_mem, 2 * xa_w)

    h = _norm(x2, norm_g3, 0, BF16)
    for l in range(depth):
        proj = _inproj(h, w_in, l)
        o_a = _hgrn(proj, lb_param, hg_norm_g3, l, batch, seq)
        o_b, w_out_b = _lru(proj, conv_w, conv_b3, wri, bri, lam3, w_out, l, batch, seq,
                            lru_x_col, lru_gate_col)
        o_c = _attn(proj, kv, l, batch, seq, n_mem, xa_q_col, xa_gate_col)
        outs = _outproj(x2, o_a, o_b, o_c, w_out_b, l, post_g3, l + 1 == depth)
        if l + 1 < depth:
            x2, h = outs
    return outs[0].reshape(batch, seq, d_model)
```

```python
import functools
import itertools

import numpy as np
import jax
import jax.numpy as jnp
from jax import lax
from jax.experimental import pallas as pl
from jax.experimental.pallas import tpu as pltpu

F32 = jnp.float32
BF16 = jnp.bfloat16

EPS = 1e-6
LRU_C = 8.0
CONV_WIDTH = 4

HG_HEADS = 8
HG_DK = 128
HG_DV = 128
LRU_BLOCKS = 8
LRU_BLOCK = 128
XA_HEADS = 4
XA_HEAD_DIM = 256

SUBLANES = 8
LANES = 128
VMEM_LIMIT_BYTES = 56 * 1024 * 1024

HG_CHUNK = 64
HG_HEADS_PER_STEP = 2
LRU_ROWS = 1024


def _half_tanh(x):
    h = 0.5 * x
    return h, jnp.tanh(h)


def _sigmoid(x):
    _, th = _half_tanh(x)
    return 0.5 + 0.5 * th


def _silu(x):
    h, th = _half_tanh(x)
    return h + h * th


def _tile(x, i):
    return x[i * SUBLANES:(i + 1) * SUBLANES]


def _bcast_row(tile, r):
    return jnp.broadcast_to(tile[r:r + 1, :], tile.shape)


def _rmsnorm_rows(x, g):
    ms = jnp.mean(x * x, axis=-1, keepdims=True)
    return x * lax.rsqrt(ms + EPS) * g


def _norm_kernel(x_ref, g_ref, o_ref):
    o_ref[...] = _rmsnorm_rows(x_ref[...], g_ref[...]).astype(o_ref.dtype)


def _norm(x2, g3, layer, dtype, tm=512):
    t, d = x2.shape
    return pl.pallas_call(
        _norm_kernel,
        grid=(t // tm,),
        in_specs=[pl.BlockSpec((tm, d), lambda i: (i, 0)),
                  pl.BlockSpec((None, 1, d), lambda i: (layer, 0, 0))],
        out_specs=pl.BlockSpec((tm, d), lambda i: (i, 0)),
        out_shape=jax.ShapeDtypeStruct((t, d), dtype),
        compiler_params=pltpu.CompilerParams(
            dimension_semantics=("parallel",),
            vmem_limit_bytes=VMEM_LIMIT_BYTES),
        name="norm",
    )(x2, g3)


def _inproj_kernel(h_ref, w_ref, o_ref):
    acc = jnp.dot(h_ref[...], w_ref[...].astype(BF16), preferred_element_type=F32)
    for c in range(o_ref.shape[0]):
        o_ref[c] = acc[:, c * LANES:(c + 1) * LANES]


def _inproj(h, w_in, layer, tm=2048, tn=512):
    t, d = h.shape
    n = w_in.shape[2]
    return pl.pallas_call(
        _inproj_kernel,
        grid=(t // tm, n // tn),
        in_specs=[
            pl.BlockSpec((tm, d), lambda i, j: (i, 0)),
            pl.BlockSpec((None, d, tn), lambda i, j: (layer, 0, j)),
        ],
        out_specs=pl.BlockSpec((tn // LANES, tm, LANES), lambda i, j: (j, i, 0)),
        out_shape=jax.ShapeDtypeStruct((n // LANES, t, LANES), F32),
        compiler_params=pltpu.CompilerParams(
            dimension_semantics=("parallel", "arbitrary"),
            vmem_limit_bytes=VMEM_LIMIT_BYTES),
        name="inproj",
    )(h, w_in)


def _hgrn_level_widths():
    w = HG_CHUNK // 2
    widths = []
    while w >= 1:
        widths.append(w)
        w //= 2
    return widths


def _hgrn_masks():
    t = np.arange(HG_CHUNK)[:, None]
    s = np.arange(HG_CHUNK)[None, :]
    masks = [((t // (2 * w)) == (s // (2 * w))) & ((t % (2 * w)) >= w) & ((s % (2 * w)) < w)
             for w in _hgrn_level_widths()]
    return np.stack(masks).astype(np.float32)


_NT = (((1,), (1,)), ((), ()))
_TN = (((0,), (0,)), ((), ()))


def _hgrn_head_steps(q_ref, f_ref, v_ref, gate_ref, lbp, ng, masks_ref, o_ref, layer):
    c = HG_CHUNK
    nt = c // SUBLANES
    widths = _hgrn_level_widths()
    n_chunks = q_ref.shape[0] // c
    lbe = jnp.exp(lbp - jnp.max(lbp, axis=0, keepdims=True))
    sm = lbe / jnp.sum(lbe, axis=0, keepdims=True)
    lb = jnp.sum(sm[1:layer + 1], axis=0, keepdims=True) if layer else jnp.zeros((1, HG_DK), F32)
    f_mid = 0.5 * (1.0 + lb)
    f_half = 0.5 * (1.0 - lb)

    row8 = lax.broadcasted_iota(jnp.int32, (SUBLANES, LANES), 0)
    scan_masks = [(k, (row8 >= k).astype(F32)) for k in (1, 2, 4)]
    second = {w: (row8 & w) != 0 for w in (4, 2, 1)}
    sign = {w: jnp.where(second[w], 1.0, -1.0) for w in (4, 2)}
    low4 = row8 < 4

    def rows(idx):
        return slice(idx * c, (idx + 1) * c)

    def front(idx):
        sl = rows(idx)
        q = q_ref[sl, :]
        v = v_ref[sl, :]
        f = f_mid + f_half * jnp.tanh(0.5 * f_ref[sl, :])
        kk = 1.0 - f
        g = jnp.log2(f)

        cum, ends = [], []
        for i in range(nt):
            t = _tile(g, i)
            for k, m in scan_masks:
                t = t + pltpu.roll(t, k, 0) * m
            if i:
                t = t + ends[-1]
            cum.append(t)
            ends.append(_bcast_row(t, SUBLANES - 1))

        xs = []
        for w in widths:
            tiles = []
            for i in range(nt):
                qt, kt, ct = _tile(q, i), _tile(kk, i), cum[i]
                if w >= SUBLANES:
                    r = i * SUBLANES
                    mid = ends[(r - r % (2 * w) + w) // SUBLANES - 1]
                    x = qt * jnp.exp2(ct - mid) if r & w else kt * jnp.exp2(mid - ct)
                elif w == 4:
                    x = jnp.where(second[4], qt, kt) * jnp.exp2((ct - _bcast_row(ct, 3)) * sign[4])
                elif w == 2:
                    mid = jnp.where(low4, _bcast_row(ct, 1), _bcast_row(ct, 5))
                    x = jnp.where(second[2], qt, kt) * jnp.exp2((ct - mid) * sign[2])
                else:
                    x = jnp.where(second[1], qt * _tile(f, i), kt)
                tiles.append(x)
            xs.append(jnp.concatenate(tiles, axis=0).astype(BF16))

        last = ends[-1]
        qi = jnp.concatenate([_tile(q, i) * jnp.exp2(cum[i]) for i in range(nt)], axis=0)
        ks = jnp.concatenate([_tile(kk, i) * jnp.exp2(last - cum[i]) for i in range(nt)], axis=0)
        dec = jnp.exp2(last)
        od = jnp.sum(q * kk, axis=-1, keepdims=True) * v
        return xs, qi.astype(BF16), ks.astype(BF16), v.astype(BF16), dec, od

    def mid_start(fr, st):
        xs, qi, ks, vb, dec, od = fr
        a = [lax.dot_general(x, x, _NT, preferred_element_type=F32) for x in xs]
        upd = lax.dot_general(vb, ks, _TN, preferred_element_type=F32)
        o_inter = lax.dot_general(qi, st.astype(BF16), _NT, preferred_element_type=F32)
        return a, upd, o_inter

    def mid_finish(fr, st, a, upd, o_inter):
        _, _, _, vb, dec, od = fr
        tiles = []
        for i in range(nt):
            r = i * SUBLANES
            acc = None
            for lvl, w in enumerate(widths):
                if w >= SUBLANES and not r & w:
                    continue
                term = _tile(a[lvl], i) * masks_ref[lvl, r:r + SUBLANES, :]
                acc = term if acc is None else acc + term
            tiles.append(acc)
        scores = jnp.concatenate(tiles, axis=0).astype(BF16)
        return (scores, o_inter + od, vb), st * dec[0:1, :] + upd

    def back_start(md):
        scores, o_partial, vb = md
        return o_partial + jnp.dot(scores, vb, preferred_element_type=F32)

    def back_finish(o, idx):
        ms = jnp.mean(o * o, axis=-1, keepdims=True)
        y = o * lax.rsqrt(ms + EPS) * ng * _silu(gate_ref[rows(idx), :])
        o_ref[rows(idx), :] = y.astype(o_ref.dtype)

    st = jnp.zeros((HG_DV, HG_DK), F32)
    fr = front(0)
    md = None
    for i in range(n_chunks):
        started = mid_start(fr, st)
        o_back = back_start(md) if md is not None else None
        nxt = front(i + 1) if i + 1 < n_chunks else None
        if o_back is not None:
            back_finish(o_back, i - 1)
        md, st = mid_finish(fr, st, *started)
        fr = nxt
        yield
    back_finish(back_start(md), n_chunks - 1)


def _hgrn_kernel(q_ref, f_ref, v_ref, gate_ref, lbp_ref, ng_ref, masks_ref, o_ref, *, layer):
    lbp = lbp_ref[...]
    ng = ng_ref[...]
    heads = []
    for hh in range(q_ref.shape[0]):
        lanes = slice(hh * LANES, (hh + 1) * LANES)
        heads.append(_hgrn_head_steps(
            q_ref.at[hh], f_ref.at[hh], v_ref.at[hh], gate_ref.at[hh], lbp[:, lanes], ng[:, lanes],
            masks_ref, o_ref.at[:, lanes], layer))
    for _ in itertools.zip_longest(*heads):
        pass


def _hgrn(proj, lb_param, ng, layer, batch, seq):
    masks = _hgrn_masks()
    t = proj.shape[1]
    nh = HG_HEADS
    hps = HG_HEADS_PER_STEP
    depth = lb_param.shape[0]

    def col(base):
        return pl.BlockSpec((hps, seq, LANES), lambda b, h: (base // hps + h, b, 0))

    return pl.pallas_call(
        functools.partial(_hgrn_kernel, layer=layer),
        grid=(batch, nh // hps),
        in_specs=[
            col(0), col(nh), col(2 * nh), col(3 * nh),
            pl.BlockSpec((depth, hps * LANES), lambda b, h: (0, h)),
            pl.BlockSpec((None, 1, hps * LANES), lambda b, h: (layer, 0, h)),
            pl.BlockSpec(masks.shape, lambda b, h: (0, 0, 0)),
        ],
        out_specs=pl.BlockSpec((seq, hps * LANES), lambda b, h: (b, h)),
        out_shape=jax.ShapeDtypeStruct((t, nh * HG_DV), BF16),
        compiler_params=pltpu.CompilerParams(
            dimension_semantics=("parallel", "parallel"),
            vmem_limit_bytes=VMEM_LIMIT_BYTES),
        name="hgrn2",
    )(proj, proj, proj, proj, lb_param, ng, jnp.asarray(masks))


def _lru_kernel(x_ref, gate_ref, cw_ref, cb_ref, wri_ref, bri_ref, lam_ref, wo_ref, o_ref, wob_ref):
    wob_ref[...] = wo_ref[...].astype(BF16)
    rws = LRU_ROWS
    nt = rws // SUBLANES
    n_steps = x_ref.shape[0] // rws
    cw = cw_ref[...]
    cb = cb_ref[...]
    bri = bri_ref[...]
    lam = lam_ref[...]
    log_a_scale = (-LRU_C) * (jnp.log1p(jnp.exp(-jnp.abs(lam))) + jnp.maximum(-lam, 0.0))
    tap = [cw[CONV_WIDTH - 1 - s:CONV_WIDTH - s, :] for s in range(CONV_WIDTH)]
    row8 = lax.broadcasted_iota(jnp.int32, (SUBLANES, LANES), 0)
    from_prev = [row8 < s for s in range(CONV_WIDTH)]
    scan_masks = [(k, row8 >= k) for k in (1, 2, 4)]

    def step(n, carry):
        prev8, hlast = carry
        r0 = pl.multiple_of(n * rws, rws)
        x = x_ref[pl.ds(r0, rws), :]

        prev_rolled = [pltpu.roll(prev8, s, 0) for s in range(1, CONV_WIDTH)]
        xc_tiles = []
        for j in range(nt):
            t = _tile(x, j)
            rolled = [pltpu.roll(t, s, 0) for s in range(1, CONV_WIDTH)]
            acc = t * tap[0] + cb
            for s in range(1, CONV_WIDTH):
                acc = acc + jnp.where(from_prev[s], prev_rolled[s - 1], rolled[s - 1]) * tap[s]
            prev_rolled = rolled
            xc_tiles.append(acc)
        xc = jnp.concatenate(xc_tiles, axis=0)

        ri = jnp.dot(xc.astype(BF16), wri_ref[...], preferred_element_type=F32) + bri
        log_a = _sigmoid(ri[:, :LANES]) * log_a_scale
        a = jnp.exp(log_a)
        u = jnp.sqrt(-jnp.tanh(log_a) * (a * a + 1.0)) * (_sigmoid(ri[:, LANES:]) * xc)

        hs = []
        for j in range(nt):
            at, ut = _tile(a, j), _tile(u, j)
            for k, m in scan_masks:
                ut = ut + at * jnp.where(m, pltpu.roll(ut, k, 0), 0.0)
                at = at * jnp.where(m, pltpu.roll(at, k, 0), 1.0)
            h8 = at * hlast + ut
            hlast = _bcast_row(h8, SUBLANES - 1)
            hs.append(h8)
        h = jnp.concatenate(hs, axis=0)

        y = h * _silu(gate_ref[pl.ds(r0, rws), :])
        o_ref[pl.ds(r0, rws), :] = y.astype(o_ref.dtype)
        return _tile(x, nt - 1), hlast

    zeros8 = jnp.zeros((SUBLANES, LANES), F32)
    lax.fori_loop(0, n_steps, step, (zeros8, zeros8))


def _lru(proj, cw, cb, wri, bri, lam, w_out, layer, batch, seq, x_col, gate_col):
    t = proj.shape[1]
    nb = LRU_BLOCKS
    k, d = w_out.shape[1:]
    wo_rows = k // (batch * nb)
    wo_spec = pl.BlockSpec((None, wo_rows, d), lambda b, n: (layer, b * nb + n, 0))
    wob_spec = pl.BlockSpec((wo_rows, d), lambda b, n: (b * nb + n, 0))

    def col(base):
        return pl.BlockSpec((None, seq, LANES), lambda b, n: (base + n, b, 0))

    def vec(rows, width=LANES):
        return pl.BlockSpec((None, rows, width), lambda b, n: (layer, 0, n))

    return pl.pallas_call(
        _lru_kernel,
        grid=(batch, nb),
        in_specs=[
            col(x_col), col(gate_col),
            vec(CONV_WIDTH), vec(1),
            pl.BlockSpec((None, None, LRU_BLOCK, 2 * LRU_BLOCK), lambda b, n: (layer, n, 0, 0)),
            vec(1, 2 * LANES),
            vec(1),
            wo_spec,
        ],
        out_specs=[pl.BlockSpec((seq, LANES), lambda b, n: (b, n)), wob_spec],
        out_shape=[jax.ShapeDtypeStruct((t, nb * LRU_BLOCK), BF16),
                   jax.ShapeDtypeStruct((k, d), BF16)],
        compiler_params=pltpu.CompilerParams(
            dimension_semantics=("parallel", "parallel"),
            vmem_limit_bytes=VMEM_LIMIT_BYTES),
        name="rglru",
    )(proj, proj, cw, cb, wri, bri, lam, w_out)


def _kv_kernel(mem_ref, g_ref, w_ref, o_ref, h_ref):
    @pl.when(pl.program_id(1) == 0)
    def _():
        h_ref[...] = _rmsnorm_rows(mem_ref[...], g_ref[...]).astype(BF16)

    o_ref[...] = jnp.dot(h_ref[...], w_ref[...].astype(BF16),
                         preferred_element_type=F32).astype(o_ref.dtype)


def _kv(mem2, g, w_kv, tn=512):
    depth, d, n = w_kv.shape
    m = mem2.shape[0]
    return pl.pallas_call(
        _kv_kernel,
        grid=(depth, n // tn),
        in_specs=[
            pl.BlockSpec((m, d), lambda l, j: (0, 0)),
            pl.BlockSpec((None, 1, d), lambda l, j: (l, 0, 0)),
            pl.BlockSpec((None, d, tn), lambda l, j: (l, 0, j)),
        ],
        out_specs=pl.BlockSpec((None, m, tn), lambda l, j: (l, 0, j)),
        out_shape=jax.ShapeDtypeStruct((depth, m, n), BF16),
        scratch_shapes=[pltpu.VMEM((m, d), BF16)],
        compiler_params=pltpu.CompilerParams(
            dimension_semantics=("parallel", "arbitrary"),
            vmem_limit_bytes=VMEM_LIMIT_BYTES),
        name="mem_kv",
    )(mem2, g, w_kv)


def _attn_kernel(q_ref, gate_ref, k_ref, v_ref, o_ref):
    scale = XA_HEAD_DIM ** -0.5
    per_head = XA_HEAD_DIM // LANES

    def head_cols(ref, h):
        return jnp.concatenate([ref[h * per_head + c] for c in range(per_head)], axis=1)

    for h in range(XA_HEADS):
        sl = slice(h * XA_HEAD_DIM, (h + 1) * XA_HEAD_DIM)
        s = lax.dot_general(head_cols(q_ref, h).astype(BF16), k_ref[:, sl], _NT,
                            preferred_element_type=F32) * scale
        m = jnp.max(s, axis=-1, keepdims=True)
        p = jnp.exp(s - m)
        p = p / jnp.sum(p, axis=-1, keepdims=True)
        o = jnp.dot(p.astype(BF16), v_ref[:, sl], preferred_element_type=F32)
        o_ref[:, sl] = (o * _silu(head_cols(gate_ref, h))).astype(o_ref.dtype)


def _attn(proj, kv, layer, batch, seq, n_mem, q_col, gate_col, ts=1024):
    t = proj.shape[1]
    width = XA_HEADS * XA_HEAD_DIM
    nt = seq // ts
    cols = width // LANES
    return pl.pallas_call(
        _attn_kernel,
        grid=(batch, nt),
        in_specs=[
            pl.BlockSpec((cols, ts, LANES), lambda b, i: (q_col, b * nt + i, 0)),
            pl.BlockSpec((cols, ts, LANES), lambda b, i: (gate_col, b * nt + i, 0)),
            pl.BlockSpec((None, n_mem, width), lambda b, i: (layer * batch + b, 0, 0)),
            pl.BlockSpec((None, n_mem, width), lambda b, i: (layer * batch + b, 0, 1)),
        ],
        out_specs=pl.BlockSpec((ts, width), lambda b, i: (b * nt + i, 0)),
        out_shape=jax.ShapeDtypeStruct((t, width), BF16),
        compiler_params=pltpu.CompilerParams(
            dimension_semantics=("parallel", "parallel"),
            vmem_limit_bytes=VMEM_LIMIT_BYTES),
        name="mem_attn",
    )(proj, proj, kv, kv)


def _outproj_kernel(x_ref, a_ref, b_ref, c_ref, w_ref, g_ref, *out_refs):
    mixed = jnp.concatenate([a_ref[...], b_ref[...], c_ref[...]], axis=1)
    acc = x_ref[...] + jnp.dot(mixed, w_ref[...], preferred_element_type=F32)
    y_ref = out_refs[-1]
    y_ref[...] = _rmsnorm_rows(acc, g_ref[...]).astype(y_ref.dtype)
    if len(out_refs) == 2:
        out_refs[0][...] = acc


def _outproj(x2, oa, ob, oc, w_bf16, layer, g3, last, tm=512):
    t, d = x2.shape
    k = w_bf16.shape[0]

    def lhs(arr):
        return pl.BlockSpec((tm, arr.shape[1]), lambda i: (i, 0))

    row_block = pl.BlockSpec((tm, d), lambda i: (i, 0))
    if last:
        out_specs = [row_block]
        out_shape = [jax.ShapeDtypeStruct((t, d), F32)]
    else:
        out_specs = [row_block, row_block]
        out_shape = [jax.ShapeDtypeStruct((t, d), F32), jax.ShapeDtypeStruct((t, d), BF16)]
    return pl.pallas_call(
        _outproj_kernel,
        grid=(t // tm,),
        in_specs=[
            row_block,
            lhs(oa), lhs(ob), lhs(oc),
            pl.BlockSpec((k, d), lambda i: (0, 0), pipeline_mode=pl.Buffered(1)),
            pl.BlockSpec((None, 1, d), lambda i: (layer, 0, 0)),
        ],
        out_specs=out_specs,
        out_shape=out_shape,
        compiler_params=pltpu.CompilerParams(
            dimension_semantics=("parallel",),
            vmem_limit_bytes=VMEM_LIMIT_BYTES),
        name="outproj",
    )(x2, oa, ob, oc, w_bf16, g3)


def kernel(x, mem, norm_g, w_in, lb_param, hg_norm_g, conv_w, conv_b, w_r, b_r, w_i, b_i, lam,
           mem_norm_g, w_kv, w_out, final_g):
    batch, seq, d_model = x.shape
    depth = w_in.shape[0]
    n_mem = mem.shape[1]
    hg_key = HG_HEADS * HG_DK
    hg_val = HG_HEADS * HG_DV
    lru_w = LRU_BLOCKS * LRU_BLOCK
    xa_w = XA_HEADS * XA_HEAD_DIM
    assert seq % HG_CHUNK == 0 and seq % LRU_ROWS == 0
    assert w_in.shape[2] == 2 * hg_key + 2 * hg_val + 2 * lru_w + 2 * xa_w
    assert hg_key == hg_val == lru_w == xa_w
    lru_x_col = (2 * hg_key + 2 * hg_val) // LANES
    lru_gate_col = lru_x_col + lru_w // LANES
    xa_q_col = (2 * hg_key + 2 * hg_val + 2 * lru_w) // xa_w
    xa_gate_col = xa_q_col + 1

    wri = jnp.concatenate([w_r, w_i], axis=-1).astype(BF16)
    bri = jnp.concatenate([b_r, b_i], axis=-1).reshape(depth, 1, -1)
    post_g3 = jnp.concatenate([norm_g[1:], final_g[None]], axis=0).reshape(depth, 1, d_model)
    norm_g3 = norm_g.reshape(depth, 1, d_model)
    hg_norm_g3 = hg_norm_g.reshape(depth, 1, hg_val)
    conv_b3 = conv_b.reshape(depth, 1, lru_w)
    lam3 = lam.reshape(depth, 1, lru_w)

    x2 = x.reshape(batch * seq, d_model)
    kv = _kv(mem.reshape(batch * n_mem, d_model), mem_norm_g.reshape(depth, 1, d_model), w_kv)
    kv = kv.reshape(depth * batch, n_mem, 2 * xa_w)

    h = _norm(x2, norm_g3, 0, BF16)
    for l in range(depth):
        proj = _inproj(h, w_in, l)
        o_a = _hgrn(proj, lb_param, hg_norm_g3, l, batch, seq)
        o_b, w_out_b = _lru(proj, conv_w, conv_b3, wri, bri, lam3, w_out, l, batch, seq,
                            lru_x_col, lru_gate_col)
        o_c = _attn(proj, kv, l, batch, seq, n_mem, xa_q_col, xa_gate_col)
        outs = _outproj(x2, o_a, o_b, o_c, w_out_b, l, post_g3, l + 1 == depth)
        if l + 1 < depth:
            x2, h = outs
    return outs[0].reshape(batch, seq, d_model)
```

```python
import functools
import itertools

import numpy as np
import jax
import jax.numpy as jnp
from jax import lax
from jax.experimental import pallas as pl
from jax.experimental.pallas import tpu as pltpu

F32 = jnp.float32
BF16 = jnp.bfloat16

EPS = 1e-6
LRU_C = 8.0
CONV_WIDTH = 4

HG_HEADS = 8
HG_DK = 128
HG_DV = 128
LRU_BLOCKS = 8
LRU_BLOCK = 128
XA_HEADS = 4
XA_HEAD_DIM = 256

SUBLANES = 8
LANES = 128
VMEM_LIMIT_BYTES = 56 * 1024 * 1024

HG_CHUNK = 64
HG_HEADS_PER_STEP = 2
LRU_ROWS = 1024
LRU_BLOCKS_PER_STEP = 2


def _half_tanh(x):
    h = 0.5 * x
    return h, jnp.tanh(h)


def _sigmoid(x):
    _, th = _half_tanh(x)
    return 0.5 + 0.5 * th


def _silu(x):
    h, th = _half_tanh(x)
    return h + h * th


def _tile(x, i):
    return x[i * SUBLANES:(i + 1) * SUBLANES]


def _bcast_row(tile, r):
    return jnp.broadcast_to(tile[r:r + 1, :], tile.shape)


def _rmsnorm_rows(x, g):
    ms = jnp.mean(x * x, axis=-1, keepdims=True)
    return x * lax.rsqrt(ms + EPS) * g


def _norm_kernel(x_ref, g_ref, o_ref):
    o_ref[...] = _rmsnorm_rows(x_ref[...], g_ref[...]).astype(o_ref.dtype)


def _norm(x2, g3, layer, dtype, tm=512):
    t, d = x2.shape
    return pl.pallas_call(
        _norm_kernel,
        grid=(t // tm,),
        in_specs=[pl.BlockSpec((tm, d), lambda i: (i, 0)),
                  pl.BlockSpec((None, 1, d), lambda i: (layer, 0, 0))],
        out_specs=pl.BlockSpec((tm, d), lambda i: (i, 0)),
        out_shape=jax.ShapeDtypeStruct((t, d), dtype),
        compiler_params=pltpu.CompilerParams(
            dimension_semantics=("parallel",),
            vmem_limit_bytes=VMEM_LIMIT_BYTES),
        name="norm",
    )(x2, g3)


def _inproj_kernel(h_ref, w_ref, o_ref):
    acc = jnp.dot(h_ref[...], w_ref[...].astype(BF16), preferred_element_type=F32)
    for c in range(o_ref.shape[0]):
        o_ref[c] = acc[:, c * LANES:(c + 1) * LANES]


def _inproj(h, w, layer=None, tm=2048):
    t, d = h.shape
    n = w.shape[-1]
    if w.ndim == 2:
        tn = 1024
        w_spec = pl.BlockSpec((d, tn), lambda i, j: (0, j))
    else:
        tn = 512
        w_spec = pl.BlockSpec((None, d, tn), lambda i, j: (layer, 0, j))
    return pl.pallas_call(
        _inproj_kernel,
        grid=(t // tm, n // tn),
        in_specs=[
            pl.BlockSpec((tm, d), lambda i, j: (i, 0)),
            w_spec,
        ],
        out_specs=pl.BlockSpec((tn // LANES, tm, LANES), lambda i, j: (j, i, 0)),
        out_shape=jax.ShapeDtypeStruct((n // LANES, t, LANES), F32),
        compiler_params=pltpu.CompilerParams(
            dimension_semantics=("parallel", "arbitrary"),
            vmem_limit_bytes=VMEM_LIMIT_BYTES),
        name="inproj",
    )(h, w)


def _hgrn_level_widths():
    w = HG_CHUNK // 2
    widths = []
    while w >= 1:
        widths.append(w)
        w //= 2
    return widths


def _hgrn_masks():
    t = np.arange(HG_CHUNK)[:, None]
    s = np.arange(HG_CHUNK)[None, :]
    masks = [((t // (2 * w)) == (s // (2 * w))) & ((t % (2 * w)) >= w) & ((s % (2 * w)) < w)
             for w in _hgrn_level_widths()]
    return np.stack(masks).astype(np.float32)


_NT = (((1,), (1,)), ((), ()))
_TN = (((0,), (0,)), ((), ()))


def _hgrn_head_steps(q_ref, f_ref, v_ref, gate_ref, lbp, ng, masks_ref, o_ref, layer):
    c = HG_CHUNK
    nt = c // SUBLANES
    widths = _hgrn_level_widths()
    n_chunks = q_ref.shape[0] // c
    lbe = jnp.exp(lbp - jnp.max(lbp, axis=0, keepdims=True))
    sm = lbe / jnp.sum(lbe, axis=0, keepdims=True)
    lb = jnp.sum(sm[1:layer + 1], axis=0, keepdims=True) if layer else jnp.zeros((1, HG_DK), F32)
    f_mid = 0.5 * (1.0 + lb)
    f_half = 0.5 * (1.0 - lb)

    row8 = lax.broadcasted_iota(jnp.int32, (SUBLANES, LANES), 0)
    scan_masks = [(k, (row8 >= k).astype(F32)) for k in (1, 2, 4)]
    second = {w: (row8 & w) != 0 for w in (4, 2, 1)}
    sign = {w: jnp.where(second[w], 1.0, -1.0) for w in (4, 2)}
    low4 = row8 < 4

    def rows(idx):
        return slice(idx * c, (idx + 1) * c)

    def front(idx):
        sl = rows(idx)
        q = q_ref[sl, :]
        v = v_ref[sl, :]
        f = f_mid + f_half * jnp.tanh(0.5 * f_ref[sl, :])
        kk = 1.0 - f
        g = jnp.log2(f)

        cum, ends = [], []
        for i in range(nt):
            t = _tile(g, i)
            for k, m in scan_masks:
                t = t + pltpu.roll(t, k, 0) * m
            if i:
                t = t + ends[-1]
            cum.append(t)
            ends.append(_bcast_row(t, SUBLANES - 1))

        xs = []
        for w in widths:
            tiles = []
            for i in range(nt):
                qt, kt, ct = _tile(q, i), _tile(kk, i), cum[i]
                if w >= SUBLANES:
                    r = i * SUBLANES
                    mid = ends[(r - r % (2 * w) + w) // SUBLANES - 1]
                    x = qt * jnp.exp2(ct - mid) if r & w else kt * jnp.exp2(mid - ct)
                elif w == 4:
                    x = jnp.where(second[4], qt, kt) * jnp.exp2((ct - _bcast_row(ct, 3)) * sign[4])
                elif w == 2:
                    mid = jnp.where(low4, _bcast_row(ct, 1), _bcast_row(ct, 5))
                    x = jnp.where(second[2], qt, kt) * jnp.exp2((ct - mid) * sign[2])
                else:
                    x = jnp.where(second[1], qt * _tile(f, i), kt)
                tiles.append(x)
            xs.append(jnp.concatenate(tiles, axis=0).astype(BF16))

        last = ends[-1]
        qi = jnp.concatenate([_tile(q, i) * jnp.exp2(cum[i]) for i in range(nt)], axis=0)
        ks = jnp.concatenate([_tile(kk, i) * jnp.exp2(last - cum[i]) for i in range(nt)], axis=0)
        dec = jnp.exp2(last)
        od = jnp.sum(q * kk, axis=-1, keepdims=True) * v
        return xs, qi.astype(BF16), ks.astype(BF16), v.astype(BF16), dec, od

    def mid_start(fr, st):
        xs, qi, ks, vb, dec, od = fr
        a = [lax.dot_general(x, x, _NT, preferred_element_type=F32) for x in xs]
        upd = lax.dot_general(vb, ks, _TN, preferred_element_type=F32)
        o_inter = lax.dot_general(qi, st.astype(BF16), _NT, preferred_element_type=F32)
        return a, upd, o_inter

    def mid_finish(fr, st, a, upd, o_inter):
        _, _, _, vb, dec, od = fr
        tiles = []
        for i in range(nt):
            r = i * SUBLANES
            acc = None
            for lvl, w in enumerate(widths):
                if w >= SUBLANES and not r & w:
                    continue
                term = _tile(a[lvl], i) * masks_ref[lvl, r:r + SUBLANES, :]
                acc = term if acc is None else acc + term
            tiles.append(acc)
        scores = jnp.concatenate(tiles, axis=0).astype(BF16)
        return (scores, o_inter + od, vb), st * dec[0:1, :] + upd

    def back_start(md):
        scores, o_partial, vb = md
        return o_partial + jnp.dot(scores, vb, preferred_element_type=F32)

    def back_finish(o, idx):
        ms = jnp.mean(o * o, axis=-1, keepdims=True)
        y = o * lax.rsqrt(ms + EPS) * ng * _silu(gate_ref[rows(idx), :])
        o_ref[rows(idx), :] = y.astype(o_ref.dtype)

    st = jnp.zeros((HG_DV, HG_DK), F32)
    fr = front(0)
    md = None
    for i in range(n_chunks):
        started = mid_start(fr, st)
        o_back = back_start(md) if md is not None else None
        nxt = front(i + 1) if i + 1 < n_chunks else None
        if o_back is not None:
            back_finish(o_back, i - 1)
        md, st = mid_finish(fr, st, *started)
        fr = nxt
        yield
    back_finish(back_start(md), n_chunks - 1)


def _hgrn_kernel(*refs, layer, side_cast):
    if side_cast:
        q_ref, f_ref, v_ref, gate_ref, lbp_ref, ng_ref, masks_ref, wi_ref, o_ref, wib_ref = refs
        wib_ref[...] = wi_ref[...].astype(BF16)
    else:
        q_ref, f_ref, v_ref, gate_ref, lbp_ref, ng_ref, masks_ref, o_ref = refs
    lbp = lbp_ref[...]
    ng = ng_ref[...]
    heads = []
    for hh in range(q_ref.shape[0]):
        lanes = slice(hh * LANES, (hh + 1) * LANES)
        heads.append(_hgrn_head_steps(
            q_ref.at[hh], f_ref.at[hh], v_ref.at[hh], gate_ref.at[hh], lbp[:, lanes], ng[:, lanes],
            masks_ref, o_ref.at[:, lanes], layer))
    for _ in itertools.zip_longest(*heads):
        pass


def _hgrn(proj, lb_param, ng, layer, batch, seq, w_in=None, cast_layer=None):
    masks = _hgrn_masks()
    t = proj.shape[1]
    nh = HG_HEADS
    hps = HG_HEADS_PER_STEP
    steps = nh // hps
    depth = lb_param.shape[0]

    def col(base):
        return pl.BlockSpec((hps, seq, LANES), lambda b, h: (base // hps + h, b, 0))

    in_specs = [
        col(0), col(nh), col(2 * nh), col(3 * nh),
        pl.BlockSpec((depth, hps * LANES), lambda b, h: (0, h)),
        pl.BlockSpec((None, 1, hps * LANES), lambda b, h: (layer, 0, h)),
        pl.BlockSpec(masks.shape, lambda b, h: (0, 0, 0)),
    ]
    out_specs = [pl.BlockSpec((seq, hps * LANES), lambda b, h: (b, h))]
    out_shape = [jax.ShapeDtypeStruct((t, nh * HG_DV), BF16)]
    args = [proj, proj, proj, proj, lb_param, ng, jnp.asarray(masks)]
    if w_in is not None:
        d, n = w_in.shape[1:]
        rows = d // (batch * steps)
        in_specs.append(pl.BlockSpec((None, rows, n), lambda b, h: (cast_layer, b * steps + h, 0)))
        out_specs.append(pl.BlockSpec((rows, n), lambda b, h: (b * steps + h, 0)))
        out_shape.append(jax.ShapeDtypeStruct((d, n), BF16))
        args.append(w_in)
    outs = pl.pallas_call(
        functools.partial(_hgrn_kernel, layer=layer, side_cast=w_in is not None),
        grid=(batch, steps),
        in_specs=in_specs,
        out_specs=out_specs,
        out_shape=out_shape,
        compiler_params=pltpu.CompilerParams(
            dimension_semantics=("parallel", "parallel"),
            vmem_limit_bytes=VMEM_LIMIT_BYTES),
        name="hgrn2",
    )(*args)
    return outs if w_in is not None else outs[0]


def _lru_block_step(x_ref, gate_ref, cw, cb, wri_ref, bri, lam, o_ref):
    rws = LRU_ROWS
    nt = rws // SUBLANES
    log_a_scale = (-LRU_C) * (jnp.log1p(jnp.exp(-jnp.abs(lam))) + jnp.maximum(-lam, 0.0))
    tap = [cw[CONV_WIDTH - 1 - s:CONV_WIDTH - s, :] for s in range(CONV_WIDTH)]
    row8 = lax.broadcasted_iota(jnp.int32, (SUBLANES, LANES), 0)
    from_prev = [row8 < s for s in range(CONV_WIDTH)]
    scan_masks = [(k, row8 >= k) for k in (1, 2, 4)]

    def step(n, carry):
        prev8, hlast = carry
        r0 = pl.multiple_of(n * rws, rws)
        x = x_ref[pl.ds(r0, rws), :]

        prev_rolled = [pltpu.roll(prev8, s, 0) for s in range(1, CONV_WIDTH)]
        xc_tiles = []
        for j in range(nt):
            t = _tile(x, j)
            rolled = [pltpu.roll(t, s, 0) for s in range(1, CONV_WIDTH)]
            acc = t * tap[0] + cb
            for s in range(1, CONV_WIDTH):
                acc = acc + jnp.where(from_prev[s], prev_rolled[s - 1], rolled[s - 1]) * tap[s]
            prev_rolled = rolled
            xc_tiles.append(acc)
        xc = jnp.concatenate(xc_tiles, axis=0)

        ri = jnp.dot(xc.astype(BF16), wri_ref[...], preferred_element_type=F32) + bri
        log_a = _sigmoid(ri[:, :LANES]) * log_a_scale
        a = jnp.exp(log_a)
        u = jnp.sqrt(-jnp.tanh(log_a) * (a * a + 1.0)) * (_sigmoid(ri[:, LANES:]) * xc)

        hs = []
        for j in range(nt):
            at, ut = _tile(a, j), _tile(u, j)
            for k, m in scan_masks:
                ut = ut + at * jnp.where(m, pltpu.roll(ut, k, 0), 0.0)
                at = at * jnp.where(m, pltpu.roll(at, k, 0), 1.0)
            h8 = at * hlast + ut
            hlast = _bcast_row(h8, SUBLANES - 1)
            hs.append(h8)
        h = jnp.concatenate(hs, axis=0)

        y = h * _silu(gate_ref[pl.ds(r0, rws), :])
        o_ref[pl.ds(r0, rws), :] = y.astype(o_ref.dtype)
        return _tile(x, nt - 1), hlast

    return step


def _lru_kernel(x_ref, gate_ref, cw_ref, cb_ref, wri_ref, bri_ref, lam_ref, wo_ref, o_ref, wob_ref):
    wob_ref[...] = wo_ref[...].astype(BF16)
    cw, cb, bri, lam = cw_ref[...], cb_ref[...], bri_ref[...], lam_ref[...]
    steps = []
    for k in range(x_ref.shape[0]):
        lanes = slice(k * LANES, (k + 1) * LANES)
        lanes2 = slice(2 * k * LANES, 2 * (k + 1) * LANES)
        steps.append(_lru_block_step(x_ref.at[k], gate_ref.at[k], cw[:, lanes], cb[:, lanes],
                                     wri_ref.at[k], bri[:, lanes2], lam[:, lanes],
                                     o_ref.at[:, lanes]))

    def body(n, carry):
        return tuple(step(n, c) for step, c in zip(steps, carry))

    zeros8 = jnp.zeros((SUBLANES, LANES), F32)
    lax.fori_loop(0, x_ref.shape[1] // LRU_ROWS, body, tuple((zeros8, zeros8) for _ in steps))


def _lru(proj, cw, cb, wri, bri, lam, w_out, layer, batch, seq, x_col, gate_col):
    t = proj.shape[1]
    nb = LRU_BLOCKS
    bps = LRU_BLOCKS_PER_STEP
    steps = nb // bps
    k, d = w_out.shape[1:]
    wo_rows = k // (batch * steps)
    wo_spec = pl.BlockSpec((None, wo_rows, d), lambda b, n: (layer, b * steps + n, 0))
    wob_spec = pl.BlockSpec((wo_rows, d), lambda b, n: (b * steps + n, 0))

    def col(base):
        return pl.BlockSpec((bps, seq, LANES), lambda b, n: (base // bps + n, b, 0))

    def vec(rows, width=LANES):
        return pl.BlockSpec((None, rows, bps * width), lambda b, n: (layer, 0, n))

    return pl.pallas_call(
        _lru_kernel,
        grid=(batch, steps),
        in_specs=[
            col(x_col), col(gate_col),
            vec(CONV_WIDTH), vec(1),
            pl.BlockSpec((None, bps, LRU_BLOCK, 2 * LRU_BLOCK), lambda b, n: (layer, n, 0, 0)),
            vec(1, 2 * LANES),
            vec(1),
            wo_spec,
        ],
        out_specs=[pl.BlockSpec((seq, bps * LANES), lambda b, n: (b, n)), wob_spec],
        out_shape=[jax.ShapeDtypeStruct((t, nb * LRU_BLOCK), BF16),
                   jax.ShapeDtypeStruct((k, d), BF16)],
        compiler_params=pltpu.CompilerParams(
            dimension_semantics=("parallel", "parallel"),
            vmem_limit_bytes=VMEM_LIMIT_BYTES),
        name="rglru",
    )(proj, proj, cw, cb, wri, bri, lam, w_out)


def _kv_kernel(mem_ref, g_ref, w_ref, o_ref, h_ref):
    @pl.when(pl.program_id(1) == 0)
    def _():
        h_ref[...] = _rmsnorm_rows(mem_ref[...], g_ref[...]).astype(BF16)

    o_ref[...] = jnp.dot(h_ref[...], w_ref[...].astype(BF16),
                         preferred_element_type=F32).astype(o_ref.dtype)


def _kv(mem2, g, w_kv, tn=512):
    depth, d, n = w_kv.shape
    m = mem2.shape[0]
    return pl.pallas_call(
        _kv_kernel,
        grid=(depth, n // tn),
        in_specs=[
            pl.BlockSpec((m, d), lambda l, j: (0, 0)),
            pl.BlockSpec((None, 1, d), lambda l, j: (l, 0, 0)),
            pl.BlockSpec((None, d, tn), lambda l, j: (l, 0, j)),
        ],
        out_specs=pl.BlockSpec((None, m, tn), lambda l, j: (l, 0, j)),
        out_shape=jax.ShapeDtypeStruct((depth, m, n), BF16),
        scratch_shapes=[pltpu.VMEM((m, d), BF16)],
        compiler_params=pltpu.CompilerParams(
            dimension_semantics=("parallel", "arbitrary"),
            vmem_limit_bytes=VMEM_LIMIT_BYTES),
        name="mem_kv",
    )(mem2, g, w_kv)


def _attn_kernel(q_ref, gate_ref, k_ref, v_ref, o_ref):
    scale = XA_HEAD_DIM ** -0.5
    per_head = XA_HEAD_DIM // LANES

    def head_cols(ref, h):
        return jnp.concatenate([ref[h * per_head + c] for c in range(per_head)], axis=1)

    for h in range(XA_HEADS):
        sl = slice(h * XA_HEAD_DIM, (h + 1) * XA_HEAD_DIM)
        s = lax.dot_general(head_cols(q_ref, h).astype(BF16), k_ref[:, sl], _NT,
                            preferred_element_type=F32) * scale
        m = jnp.max(s, axis=-1, keepdims=True)
        p = jnp.exp(s - m)
        p = p / jnp.sum(p, axis=-1, keepdims=True)
        o = jnp.dot(p.astype(BF16), v_ref[:, sl], preferred_element_type=F32)
        o_ref[:, sl] = (o * _silu(head_cols(gate_ref, h))).astype(o_ref.dtype)


def _attn(proj, kv, layer, batch, seq, n_mem, q_col, gate_col, ts=1024):
    t = proj.shape[1]
    width = XA_HEADS * XA_HEAD_DIM
    nt = seq // ts
    cols = width // LANES
    return pl.pallas_call(
        _attn_kernel,
        grid=(batch, nt),
        in_specs=[
            pl.BlockSpec((cols, ts, LANES), lambda b, i: (q_col, b * nt + i, 0)),
            pl.BlockSpec((cols, ts, LANES), lambda b, i: (gate_col, b * nt + i, 0)),
            pl.BlockSpec((None, n_mem, width), lambda b, i: (layer * batch + b, 0, 0)),
            pl.BlockSpec((None, n_mem, width), lambda b, i: (layer * batch + b, 0, 1)),
        ],
        out_specs=pl.BlockSpec((ts, width), lambda b, i: (b * nt + i, 0)),
        out_shape=jax.ShapeDtypeStruct((t, width), BF16),
        compiler_params=pltpu.CompilerParams(
            dimension_semantics=("parallel", "parallel"),
            vmem_limit_bytes=VMEM_LIMIT_BYTES),
        name="mem_attn",
    )(proj, proj, kv, kv)


def _outproj_kernel(x_ref, a_ref, b_ref, c_ref, w_ref, g_ref, *out_refs):
    mixed = jnp.concatenate([a_ref[...], b_ref[...], c_ref[...]], axis=1)
    acc = x_ref[...] + jnp.dot(mixed, w_ref[...], preferred_element_type=F32)
    y_ref = out_refs[-1]
    y_ref[...] = _rmsnorm_rows(acc, g_ref[...]).astype(y_ref.dtype)
    if len(out_refs) == 2:
        out_refs[0][...] = acc


def _outproj(x2, oa, ob, oc, w_bf16, layer, g3, last, tm=512):
    t, d = x2.shape
    k = w_bf16.shape[0]

    def lhs(arr):
        return pl.BlockSpec((tm, arr.shape[1]), lambda i: (i, 0))

    row_block = pl.BlockSpec((tm, d), lambda i: (i, 0))
    if last:
        out_specs = [row_block]
        out_shape = [jax.ShapeDtypeStruct((t, d), F32)]
    else:
        out_specs = [row_block, row_block]
        out_shape = [jax.ShapeDtypeStruct((t, d), F32), jax.ShapeDtypeStruct((t, d), BF16)]
    return pl.pallas_call(
        _outproj_kernel,
        grid=(t // tm,),
        in_specs=[
            row_block,
            lhs(oa), lhs(ob), lhs(oc),
            pl.BlockSpec((k, d), lambda i: (0, 0), pipeline_mode=pl.Buffered(1)),
            pl.BlockSpec((None, 1, d), lambda i: (layer, 0, 0)),
        ],
        out_specs=out_specs,
        out_shape=out_shape,
        compiler_params=pltpu.CompilerParams(
            dimension_semantics=("parallel",),
            vmem_limit_bytes=VMEM_LIMIT_BYTES),
        name="outproj",
    )(x2, oa, ob, oc, w_bf16, g3)


def kernel(x, mem, norm_g, w_in, lb_param, hg_norm_g, conv_w, conv_b, w_r, b_r, w_i, b_i, lam,
           mem_norm_g, w_kv, w_out, final_g):
    batch, seq, d_model = x.shape
    depth = w_in.shape[0]
    n_mem = mem.shape[1]
    hg_key = HG_HEADS * HG_DK
    hg_val = HG_HEADS * HG_DV
    lru_w = LRU_BLOCKS * LRU_BLOCK
    xa_w = XA_HEADS * XA_HEAD_DIM
    assert seq % HG_CHUNK == 0 and seq % LRU_ROWS == 0
    assert w_in.shape[2] == 2 * hg_key + 2 * hg_val + 2 * lru_w + 2 * xa_w
    assert hg_key == hg_val == lru_w == xa_w
    lru_x_col = (2 * hg_key + 2 * hg_val) // LANES
    lru_gate_col = lru_x_col + lru_w // LANES
    xa_q_col = (2 * hg_key + 2 * hg_val + 2 * lru_w) // xa_w
    xa_gate_col = xa_q_col + 1

    wri = jnp.concatenate([w_r, w_i], axis=-1).astype(BF16)
    bri = jnp.concatenate([b_r, b_i], axis=-1).reshape(depth, 1, -1)
    post_g3 = jnp.concatenate([norm_g[1:], final_g[None]], axis=0).reshape(depth, 1, d_model)
    norm_g3 = norm_g.reshape(depth, 1, d_model)
    hg_norm_g3 = hg_norm_g.reshape(depth, 1, hg_val)
    conv_b3 = conv_b.reshape(depth, 1, lru_w)
    lam3 = lam.reshape(depth, 1, lru_w)

    x2 = x.reshape(batch * seq, d_model)
    kv = _kv(mem.reshape(batch * n_mem, d_model), mem_norm_g.reshape(depth, 1, d_model), w_kv)
    kv = kv.reshape(depth * batch, n_mem, 2 * xa_w)

    h = _norm(x2, norm_g3, 0, BF16)
    w_in_b = None
    for l in range(depth):
        proj = _inproj(h, w_in, l) if w_in_b is None else _inproj(h, w_in_b)
        if l + 1 < depth:
            o_a, w_in_b = _hgrn(proj, lb_param, hg_norm_g3, l, batch, seq, w_in, l + 1)
        else:
            o_a = _hgrn(proj, lb_param, hg_norm_g3, l, batch, seq)
        o_b, w_out_b = _lru(proj, conv_w, conv_b3, wri, bri, lam3, w_out, l, batch, seq,
                            lru_x_col, lru_gate_col)
        o_c = _attn(proj, kv, l, batch, seq, n_mem, xa_q_col, xa_gate_col)
        outs = _outproj(x2, o_a, o_b, o_c, w_out_b, l, post_g3, l + 1 == depth)
        if l + 1 < depth:
            x2, h = outs
    return outs[0].reshape(batch, seq, d_model)
```

```python
import functools
import itertools
import math

import numpy as np
import jax
import jax.numpy as jnp
from jax import lax
from jax.experimental import pallas as pl
from jax.experimental.pallas import tpu as pltpu

F32 = jnp.float32
BF16 = jnp.bfloat16

EPS = 1e-6
LRU_C = 8.0
CONV_WIDTH = 4

HG_HEADS = 8
HG_DK = 128
HG_DV = 128
LRU_BLOCKS = 8
LRU_BLOCK = 128
XA_HEADS = 4
XA_HEAD_DIM = 256

SUBLANES = 8
LANES = 128
VMEM_LIMIT_BYTES = 56 * 1024 * 1024

HG_CHUNK = 64
HG_HEADS_PER_STEP = 2
LRU_ROWS = 1024
LRU_BLOCKS_PER_STEP = 2


def _silu(x):
    h = 0.5 * x
    return h + h * jnp.tanh(h)


def _tile(x, i):
    return x[i * SUBLANES:(i + 1) * SUBLANES]


def _bcast_row(tile, r):
    return jnp.broadcast_to(tile[r:r + 1, :], tile.shape)


def _rmsnorm_rows(x, g):
    ms = jnp.mean(x * x, axis=-1, keepdims=True)
    return x * lax.rsqrt(ms + EPS) * g


def _norm_kernel(x_ref, g_ref, o_ref):
    o_ref[...] = _rmsnorm_rows(x_ref[...], g_ref[...]).astype(o_ref.dtype)


def _norm(x2, g3, layer, dtype, tm=512):
    t, d = x2.shape
    return pl.pallas_call(
        _norm_kernel,
        grid=(t // tm,),
        in_specs=[pl.BlockSpec((tm, d), lambda i: (i, 0)),
                  pl.BlockSpec((None, 1, d), lambda i: (layer, 0, 0))],
        out_specs=pl.BlockSpec((tm, d), lambda i: (i, 0)),
        out_shape=jax.ShapeDtypeStruct((t, d), dtype),
        compiler_params=pltpu.CompilerParams(
            dimension_semantics=("parallel",),
            vmem_limit_bytes=VMEM_LIMIT_BYTES),
        name="norm",
    )(x2, g3)


def _inproj_kernel(h_ref, w_ref, o_ref):
    acc = jnp.dot(h_ref[...], w_ref[...].astype(BF16), preferred_element_type=F32)
    for c in range(o_ref.shape[0]):
        o_ref[c] = acc[:, c * LANES:(c + 1) * LANES]


def _inproj(h, w, layer=None, tm=2048):
    t, d = h.shape
    n = w.shape[-1]
    if w.ndim == 2:
        tn = 1024
        w_spec = pl.BlockSpec((d, tn), lambda i, j: (0, j))
    else:
        tn = 512
        w_spec = pl.BlockSpec((None, d, tn), lambda i, j: (layer, 0, j))
    return pl.pallas_call(
        _inproj_kernel,
        grid=(t // tm, n // tn),
        in_specs=[
            pl.BlockSpec((tm, d), lambda i, j: (i, 0)),
            w_spec,
        ],
        out_specs=pl.BlockSpec((tn // LANES, tm, LANES), lambda i, j: (j, i, 0)),
        out_shape=jax.ShapeDtypeStruct((n // LANES, t, LANES), F32),
        compiler_params=pltpu.CompilerParams(
            dimension_semantics=("parallel", "arbitrary"),
            vmem_limit_bytes=VMEM_LIMIT_BYTES),
        name="inproj",
    )(h, w)


def _hgrn_level_widths():
    w = HG_CHUNK // 2
    widths = []
    while w >= 1:
        widths.append(w)
        w //= 2
    return widths


def _hgrn_masks():
    t = np.arange(HG_CHUNK)[:, None]
    s = np.arange(HG_CHUNK)[None, :]
    masks = [((t // (2 * w)) == (s // (2 * w))) & ((t % (2 * w)) >= w) & ((s % (2 * w)) < w)
             for w in _hgrn_level_widths()]
    return np.stack(masks).astype(np.float32)


_NT = (((1,), (1,)), ((), ()))
_TN = (((0,), (0,)), ((), ()))


def _hgrn_head_steps(q_ref, f_ref, v_ref, gate_ref, lbp, ng, masks_ref, o_ref, layer):
    c = HG_CHUNK
    nt = c // SUBLANES
    widths = _hgrn_level_widths()
    n_chunks = q_ref.shape[0] // c
    lbe = jnp.exp(lbp - jnp.max(lbp, axis=0, keepdims=True))
    sm = lbe / jnp.sum(lbe, axis=0, keepdims=True)
    lb = jnp.sum(sm[1:layer + 1], axis=0, keepdims=True) if layer else jnp.zeros((1, HG_DK), F32)
    f_mid = 0.5 * (1.0 + lb)
    f_half = 0.5 * (1.0 - lb)

    row8 = lax.broadcasted_iota(jnp.int32, (SUBLANES, LANES), 0)
    scan_masks = [(k, (row8 >= k).astype(F32)) for k in (1, 2, 4)]
    second = {w: (row8 & w) != 0 for w in (4, 2, 1)}
    sign = {w: jnp.where(second[w], 1.0, -1.0) for w in (4, 2)}
    low4 = row8 < 4

    def rows(idx):
        return slice(idx * c, (idx + 1) * c)

    def front(idx):
        sl = rows(idx)
        q = q_ref[sl, :]
        v = v_ref[sl, :]
        f = f_mid + f_half * jnp.tanh(0.5 * f_ref[sl, :])
        kk = 1.0 - f
        g = jnp.log2(f)

        cum, ends = [], []
        for i in range(nt):
            t = _tile(g, i)
            for k, m in scan_masks:
                t = t + pltpu.roll(t, k, 0) * m
            if i:
                t = t + ends[-1]
            cum.append(t)
            ends.append(_bcast_row(t, SUBLANES - 1))

        xs = []
        for w in widths:
            tiles = []
            for i in range(nt):
                qt, kt, ct = _tile(q, i), _tile(kk, i), cum[i]
                if w >= SUBLANES:
                    r = i * SUBLANES
                    mid = ends[(r - r % (2 * w) + w) // SUBLANES - 1]
                    x = qt * jnp.exp2(ct - mid) if r & w else kt * jnp.exp2(mid - ct)
                elif w == 4:
                    x = jnp.where(second[4], qt, kt) * jnp.exp2((ct - _bcast_row(ct, 3)) * sign[4])
                elif w == 2:
                    mid = jnp.where(low4, _bcast_row(ct, 1), _bcast_row(ct, 5))
                    x = jnp.where(second[2], qt, kt) * jnp.exp2((ct - mid) * sign[2])
                else:
                    x = jnp.where(second[1], qt * _tile(f, i), kt)
                tiles.append(x)
            xs.append(jnp.concatenate(tiles, axis=0).astype(BF16))

        last = ends[-1]
        qi = jnp.concatenate([_tile(q, i) * jnp.exp2(cum[i]) for i in range(nt)], axis=0)
        ks = jnp.concatenate([_tile(kk, i) * jnp.exp2(last - cum[i]) for i in range(nt)], axis=0)
        dec = jnp.exp2(last)
        od = jnp.sum(q * kk, axis=-1, keepdims=True) * v
        return xs, qi.astype(BF16), ks.astype(BF16), v.astype(BF16), dec, od

    def mid_start(fr, st):
        xs, qi, ks, vb, dec, od = fr
        a = [lax.dot_general(x, x, _NT, preferred_element_type=F32) for x in xs]
        upd = lax.dot_general(vb, ks, _TN, preferred_element_type=F32)
        o_inter = lax.dot_general(qi, st.astype(BF16), _NT, preferred_element_type=F32)
        return a, upd, o_inter

    def mid_finish(fr, st, a, upd, o_inter):
        _, _, _, vb, dec, od = fr
        tiles = []
        for i in range(nt):
            r = i * SUBLANES
            acc = None
            for lvl, w in enumerate(widths):
                if w >= SUBLANES and not r & w:
                    continue
                term = _tile(a[lvl], i) * masks_ref[lvl, r:r + SUBLANES, :]
                acc = term if acc is None else acc + term
            tiles.append(acc)
        scores = jnp.concatenate(tiles, axis=0).astype(BF16)
        return (scores, o_inter + od, vb), st * dec[0:1, :] + upd

    def back_start(md):
        scores, o_partial, vb = md
        return o_partial + jnp.dot(scores, vb, preferred_element_type=F32)

    def back_finish(o, idx):
        ms = jnp.mean(o * o, axis=-1, keepdims=True)
        y = o * lax.rsqrt(ms + EPS) * ng * _silu(gate_ref[rows(idx), :])
        o_ref[rows(idx), :] = y.astype(o_ref.dtype)

    st = jnp.zeros((HG_DV, HG_DK), F32)
    fr = front(0)
    md = None
    for i in range(n_chunks):
        started = mid_start(fr, st)
        o_back = back_start(md) if md is not None else None
        nxt = front(i + 1) if i + 1 < n_chunks else None
        if o_back is not None:
            back_finish(o_back, i - 1)
        md, st = mid_finish(fr, st, *started)
        fr = nxt
        yield
    back_finish(back_start(md), n_chunks - 1)


def _hgrn_kernel(*refs, layer, side_cast):
    if side_cast:
        q_ref, f_ref, v_ref, gate_ref, lbp_ref, ng_ref, masks_ref, wi_ref, o_ref, wib_ref = refs
        wib_ref[...] = wi_ref[...].astype(BF16)
    else:
        q_ref, f_ref, v_ref, gate_ref, lbp_ref, ng_ref, masks_ref, o_ref = refs
    lbp = lbp_ref[...]
    ng = ng_ref[...]
    heads = []
    for hh in range(q_ref.shape[0]):
        lanes = slice(hh * LANES, (hh + 1) * LANES)
        heads.append(_hgrn_head_steps(
            q_ref.at[hh], f_ref.at[hh], v_ref.at[hh], gate_ref.at[hh], lbp[:, lanes], ng[:, lanes],
            masks_ref, o_ref.at[:, lanes], layer))
    for _ in itertools.zip_longest(*heads):
        pass


def _hgrn(proj, lb_param, ng, layer, batch, seq, w_in=None, cast_layer=None):
    masks = _hgrn_masks()
    t = proj.shape[1]
    nh = HG_HEADS
    hps = HG_HEADS_PER_STEP
    steps = nh // hps
    depth = lb_param.shape[0]

    def col(base):
        return pl.BlockSpec((hps, seq, LANES), lambda b, h: (base // hps + h, b, 0))

    in_specs = [
        col(0), col(nh), col(2 * nh), col(3 * nh),
        pl.BlockSpec((depth, hps * LANES), lambda b, h: (0, h)),
        pl.BlockSpec((None, 1, hps * LANES), lambda b, h: (layer, 0, h)),
        pl.BlockSpec(masks.shape, lambda b, h: (0, 0, 0)),
    ]
    out_specs = [pl.BlockSpec((seq, hps * LANES), lambda b, h: (b, h))]
    out_shape = [jax.ShapeDtypeStruct((t, nh * HG_DV), BF16)]
    args = [proj, proj, proj, proj, lb_param, ng, jnp.asarray(masks)]
    if w_in is not None:
        d, n = w_in.shape[1:]
        rows = d // (batch * steps)
        in_specs.append(pl.BlockSpec((None, rows, n), lambda b, h: (cast_layer, b * steps + h, 0)))
        out_specs.append(pl.BlockSpec((rows, n), lambda b, h: (b * steps + h, 0)))
        out_shape.append(jax.ShapeDtypeStruct((d, n), BF16))
        args.append(w_in)
    outs = pl.pallas_call(
        functools.partial(_hgrn_kernel, layer=layer, side_cast=w_in is not None),
        grid=(batch, steps),
        in_specs=in_specs,
        out_specs=out_specs,
        out_shape=out_shape,
        compiler_params=pltpu.CompilerParams(
            dimension_semantics=("parallel", "parallel"),
            vmem_limit_bytes=VMEM_LIMIT_BYTES),
        name="hgrn2",
    )(*args)
    return outs if w_in is not None else outs[0]


def _lru_block_step(x_ref, gate_ref, cw, cb, wri_ref, bri, lam, o_ref):
    rws = LRU_ROWS
    nt = rws // SUBLANES
    nla_half = (0.5 * LRU_C) * (jnp.log1p(jnp.exp(-jnp.abs(lam))) + jnp.maximum(-lam, 0.0))
    tap = [cw[CONV_WIDTH - 1 - s:CONV_WIDTH - s, :] for s in range(CONV_WIDTH)]
    row8 = lax.broadcasted_iota(jnp.int32, (SUBLANES, LANES), 0)
    from_prev = [row8 < s for s in range(CONV_WIDTH)]
    scan_masks = [(k, row8 >= k) for k in (1, 2, 4)]

    def step(n, carry):
        prev8, hlast = carry
        r0 = pl.multiple_of(n * rws, rws)
        x = x_ref[pl.ds(r0, rws), :]

        prev_rolled = [pltpu.roll(prev8, s, 0) for s in range(1, CONV_WIDTH)]
        xc_tiles = []
        for j in range(nt):
            t = _tile(x, j)
            rolled = [pltpu.roll(t, s, 0) for s in range(1, CONV_WIDTH)]
            acc = t * tap[0] + cb
            for s in range(1, CONV_WIDTH):
                acc = acc + jnp.where(from_prev[s], prev_rolled[s - 1], rolled[s - 1]) * tap[s]
            prev_rolled = rolled
            xc_tiles.append(acc)
        xc = jnp.concatenate(xc_tiles, axis=0)

        th = jnp.tanh(jnp.dot(xc.astype(BF16), wri_ref[...], preferred_element_type=F32) + bri)
        nla = nla_half + nla_half * th[:, :LANES]
        a = jnp.exp2(nla * (-1.0 / math.log(2.0)))
        one_minus_a2 = jnp.tanh(nla) * (a * a + 1.0)
        root = jnp.where(one_minus_a2 > 0.0, one_minus_a2 * lax.rsqrt(one_minus_a2), 0.0)
        half_xc = 0.5 * xc
        u = root * (half_xc + half_xc * th[:, LANES:])

        hs = []
        for j in range(nt):
            at, ut = _tile(a, j), _tile(u, j)
            for k, m in scan_masks:
                ut = ut + at * jnp.where(m, pltpu.roll(ut, k, 0), 0.0)
                at = at * jnp.where(m, pltpu.roll(at, k, 0), 1.0)
            h8 = at * hlast + ut
            hlast = _bcast_row(h8, SUBLANES - 1)
            hs.append(h8)
        h = jnp.concatenate(hs, axis=0)

        y = h * _silu(gate_ref[pl.ds(r0, rws), :])
        o_ref[pl.ds(r0, rws), :] = y.astype(o_ref.dtype)
        return _tile(x, nt - 1), hlast

    return step


def _lru_kernel(x_ref, gate_ref, cw_ref, cb_ref, wri_ref, bri_ref, lam_ref, wo_ref, o_ref, wob_ref):
    wob_ref[...] = wo_ref[...].astype(BF16)
    cw, cb, bri, lam = cw_ref[...], cb_ref[...], bri_ref[...], lam_ref[...]
    steps = []
    for k in range(x_ref.shape[0]):
        lanes = slice(k * LANES, (k + 1) * LANES)
        lanes2 = slice(2 * k * LANES, 2 * (k + 1) * LANES)
        steps.append(_lru_block_step(x_ref.at[k], gate_ref.at[k], cw[:, lanes], cb[:, lanes],
                                     wri_ref.at[k], bri[:, lanes2], lam[:, lanes],
                                     o_ref.at[:, lanes]))

    def body(n, carry):
        return tuple(step(n, c) for step, c in zip(steps, carry))

    zeros8 = jnp.zeros((SUBLANES, LANES), F32)
    lax.fori_loop(0, x_ref.shape[1] // LRU_ROWS, body, tuple((zeros8, zeros8) for _ in steps))


def _lru(proj, cw, cb, wri, bri, lam, w_out, layer, batch, seq, x_col, gate_col):
    t = proj.shape[1]
    nb = LRU_BLOCKS
    bps = LRU_BLOCKS_PER_STEP
    steps = nb // bps
    k, d = w_out.shape[1:]
    wo_rows = k // (batch * steps)
    wo_spec = pl.BlockSpec((None, wo_rows, d), lambda b, n: (layer, b * steps + n, 0))
    wob_spec = pl.BlockSpec((wo_rows, d), lambda b, n: (b * steps + n, 0))

    def col(base):
        return pl.BlockSpec((bps, seq, LANES), lambda b, n: (base // bps + n, b, 0))

    def vec(rows, width=LANES):
        return pl.BlockSpec((None, rows, bps * width), lambda b, n: (layer, 0, n))

    return pl.pallas_call(
        _lru_kernel,
        grid=(batch, steps),
        in_specs=[
            col(x_col), col(gate_col),
            vec(CONV_WIDTH), vec(1),
            pl.BlockSpec((None, bps, LRU_BLOCK, 2 * LRU_BLOCK), lambda b, n: (layer, n, 0, 0)),
            vec(1, 2 * LANES),
            vec(1),
            wo_spec,
        ],
        out_specs=[pl.BlockSpec((seq, bps * LANES), lambda b, n: (b, n)), wob_spec],
        out_shape=[jax.ShapeDtypeStruct((t, nb * LRU_BLOCK), BF16),
                   jax.ShapeDtypeStruct((k, d), BF16)],
        compiler_params=pltpu.CompilerParams(
            dimension_semantics=("parallel", "parallel"),
            vmem_limit_bytes=VMEM_LIMIT_BYTES),
        name="rglru",
    )(proj, proj, cw, cb, wri, bri, lam, w_out)


def _kv_kernel(mem_ref, g_ref, w_ref, o_ref, h_ref):
    @pl.when(pl.program_id(1) == 0)
    def _():
        h_ref[...] = _rmsnorm_rows(mem_ref[...], g_ref[...]).astype(BF16)

    o_ref[...] = jnp.dot(h_ref[...], w_ref[...].astype(BF16),
                         preferred_element_type=F32).astype(o_ref.dtype)


def _kv(mem2, g, w_kv, tn=512):
    depth, d, n = w_kv.shape
    m = mem2.shape[0]
    return pl.pallas_call(
        _kv_kernel,
        grid=(depth, n // tn),
        in_specs=[
            pl.BlockSpec((m, d), lambda l, j: (0, 0)),
            pl.BlockSpec((None, 1, d), lambda l, j: (l, 0, 0)),
            pl.BlockSpec((None, d, tn), lambda l, j: (l, 0, j)),
        ],
        out_specs=pl.BlockSpec((None, m, tn), lambda l, j: (l, 0, j)),
        out_shape=jax.ShapeDtypeStruct((depth, m, n), BF16),
        scratch_shapes=[pltpu.VMEM((m, d), BF16)],
        compiler_params=pltpu.CompilerParams(
            dimension_semantics=("parallel", "arbitrary"),
            vmem_limit_bytes=VMEM_LIMIT_BYTES),
        name="mem_kv",
    )(mem2, g, w_kv)


def _attn_kernel(q_ref, gate_ref, k_ref, v_ref, o_ref):
    scale = XA_HEAD_DIM ** -0.5
    per_head = XA_HEAD_DIM // LANES

    def head_cols(ref, h):
        return jnp.concatenate([ref[h * per_head + c] for c in range(per_head)], axis=1)

    for h in range(XA_HEADS):
        sl = slice(h * XA_HEAD_DIM, (h + 1) * XA_HEAD_DIM)
        s = lax.dot_general(head_cols(q_ref, h).astype(BF16), k_ref[:, sl], _NT,
                            preferred_element_type=F32) * scale
        m = jnp.max(s, axis=-1, keepdims=True)
        p = jnp.exp(s - m)
        p = p / jnp.sum(p, axis=-1, keepdims=True)
        o = jnp.dot(p.astype(BF16), v_ref[:, sl], preferred_element_type=F32)
        o_ref[:, sl] = (o * _silu(head_cols(gate_ref, h))).astype(o_ref.dtype)


def _attn(proj, kv, layer, batch, seq, n_mem, q_col, gate_col, ts=1024):
    t = proj.shape[1]
    width = XA_HEADS * XA_HEAD_DIM
    nt = seq // ts
    cols = width // LANES
    return pl.pallas_call(
        _attn_kernel,
        grid=(batch, nt),
        in_specs=[
            pl.BlockSpec((cols, ts, LANES), lambda b, i: (q_col, b * nt + i, 0)),
            pl.BlockSpec((cols, ts, LANES), lambda b, i: (gate_col, b * nt + i, 0)),
            pl.BlockSpec((None, n_mem, width), lambda b, i: (layer * batch + b, 0, 0)),
            pl.BlockSpec((None, n_mem, width), lambda b, i: (layer * batch + b, 0, 1)),
        ],
        out_specs=pl.BlockSpec((ts, width), lambda b, i: (b * nt + i, 0)),
        out_shape=jax.ShapeDtypeStruct((t, width), BF16),
        compiler_params=pltpu.CompilerParams(
            dimension_semantics=("parallel", "parallel"),
            vmem_limit_bytes=VMEM_LIMIT_BYTES),
        name="mem_attn",
    )(proj, proj, kv, kv)


def _outproj_kernel(x_ref, a_ref, b_ref, c_ref, w_ref, g_ref, *out_refs):
    mixed = jnp.concatenate([a_ref[...], b_ref[...], c_ref[...]], axis=1)
    acc = x_ref[...] + jnp.dot(mixed, w_ref[...], preferred_element_type=F32)
    y_ref = out_refs[-1]
    y_ref[...] = _rmsnorm_rows(acc, g_ref[...]).astype(y_ref.dtype)
    if len(out_refs) == 2:
        out_refs[0][...] = acc


def _outproj(x2, oa, ob, oc, w_bf16, layer, g3, last, tm=512):
    t, d = x2.shape
    k = w_bf16.shape[0]

    def lhs(arr):
        return pl.BlockSpec((tm, arr.shape[1]), lambda i: (i, 0))

    row_block = pl.BlockSpec((tm, d), lambda i: (i, 0))
    if last:
        out_specs = [row_block]
        out_shape = [jax.ShapeDtypeStruct((t, d), F32)]
    else:
        out_specs = [row_block, row_block]
        out_shape = [jax.ShapeDtypeStruct((t, d), F32), jax.ShapeDtypeStruct((t, d), BF16)]
    return pl.pallas_call(
        _outproj_kernel,
        grid=(t // tm,),
        in_specs=[
            row_block,
            lhs(oa), lhs(ob), lhs(oc),
            pl.BlockSpec((k, d), lambda i: (0, 0), pipeline_mode=pl.Buffered(1)),
            pl.BlockSpec((None, 1, d), lambda i: (layer, 0, 0)),
        ],
        out_specs=out_specs,
        out_shape=out_shape,
        compiler_params=pltpu.CompilerParams(
            dimension_semantics=("parallel",),
            vmem_limit_bytes=VMEM_LIMIT_BYTES),
        name="outproj",
    )(x2, oa, ob, oc, w_bf16, g3)


def kernel(x, mem, norm_g, w_in, lb_param, hg_norm_g, conv_w, conv_b, w_r, b_r, w_i, b_i, lam,
           mem_norm_g, w_kv, w_out, final_g):
    batch, seq, d_model = x.shape
    depth = w_in.shape[0]
    n_mem = mem.shape[1]
    hg_key = HG_HEADS * HG_DK
    hg_val = HG_HEADS * HG_DV
    lru_w = LRU_BLOCKS * LRU_BLOCK
    xa_w = XA_HEADS * XA_HEAD_DIM
    assert seq % HG_CHUNK == 0 and seq % LRU_ROWS == 0
    assert w_in.shape[2] == 2 * hg_key + 2 * hg_val + 2 * lru_w + 2 * xa_w
    assert hg_key == hg_val == lru_w == xa_w
    lru_x_col = (2 * hg_key + 2 * hg_val) // LANES
    lru_gate_col = lru_x_col + lru_w // LANES
    xa_q_col = (2 * hg_key + 2 * hg_val + 2 * lru_w) // xa_w
    xa_gate_col = xa_q_col + 1

    wri = (0.5 * jnp.concatenate([w_r, w_i], axis=-1)).astype(BF16)
    bri = (0.5 * jnp.concatenate([b_r, b_i], axis=-1)).reshape(depth, 1, -1)
    post_g3 = jnp.concatenate([norm_g[1:], final_g[None]], axis=0).reshape(depth, 1, d_model)
    norm_g3 = norm_g.reshape(depth, 1, d_model)
    hg_norm_g3 = hg_norm_g.reshape(depth, 1, hg_val)
    conv_b3 = conv_b.reshape(depth, 1, lru_w)
    lam3 = lam.reshape(depth, 1, lru_w)

    x2 = x.reshape(batch * seq, d_model)
    kv = _kv(mem.reshape(batch * n_mem, d_model), mem_norm_g.reshape(depth, 1, d_model), w_kv)
    kv = kv.reshape(depth * batch, n_mem, 2 * xa_w)

    h = _norm(x2, norm_g3, 0, BF16)
    w_in_b = None
    for l in range(depth):
        proj = _inproj(h, w_in, l) if w_in_b is None else _inproj(h, w_in_b)
        if l + 1 < depth:
            o_a, w_in_b = _hgrn(proj, lb_param, hg_norm_g3, l, batch, seq, w_in, l + 1)
        else:
            o_a = _hgrn(proj, lb_param, hg_norm_g3, l, batch, seq)
        o_b, w_out_b = _lru(proj, conv_w, conv_b3, wri, bri, lam3, w_out, l, batch, seq,
                            lru_x_col, lru_gate_col)
        o_c = _attn(proj, kv, l, batch, seq, n_mem, xa_q_col, xa_gate_col)
        outs = _outproj(x2, o_a, o_b, o_c, w_out_b, l, post_g3, l + 1 == depth)
        if l + 1 < depth:
            x2, h = outs
    return outs[0].reshape(batch, seq, d_model)
```

```python
import functools
import itertools
import math

import numpy as np
import jax
import jax.numpy as jnp
from jax import lax
from jax.experimental import pallas as pl
from jax.experimental.pallas import tpu as pltpu

F32 = jnp.float32
BF16 = jnp.bfloat16

EPS = 1e-6
LRU_C = 8.0
CONV_WIDTH = 4

HG_HEADS = 8
HG_DK = 128
HG_DV = 128
LRU_BLOCKS = 8
LRU_BLOCK = 128
XA_HEADS = 4
XA_HEAD_DIM = 256

SUBLANES = 8
LANES = 128
VMEM_LIMIT_BYTES = 56 * 1024 * 1024

HG_CHUNK = 64
HG_HEADS_PER_STEP = 2
LRU_ROWS = 1024
LRU_BLOCKS_PER_STEP = 2


def _silu(x):
    h = 0.5 * x
    return h + h * jnp.tanh(h)


def _tile(x, i):
    return x[i * SUBLANES:(i + 1) * SUBLANES]


def _bcast_row(tile, r):
    return jnp.broadcast_to(tile[r:r + 1, :], tile.shape)


def _rmsnorm_rows(x, g):
    ms = jnp.mean(x * x, axis=-1, keepdims=True)
    return x * lax.rsqrt(ms + EPS) * g


def _norm_kernel(x_ref, g_ref, o_ref):
    o_ref[...] = _rmsnorm_rows(x_ref[...], g_ref[...]).astype(o_ref.dtype)


def _norm(x2, g3, layer, dtype, tm=512):
    t, d = x2.shape
    return pl.pallas_call(
        _norm_kernel,
        grid=(t // tm,),
        in_specs=[pl.BlockSpec((tm, d), lambda i: (i, 0)),
                  pl.BlockSpec((None, 1, d), lambda i: (layer, 0, 0))],
        out_specs=pl.BlockSpec((tm, d), lambda i: (i, 0)),
        out_shape=jax.ShapeDtypeStruct((t, d), dtype),
        compiler_params=pltpu.CompilerParams(
            dimension_semantics=("parallel",),
            vmem_limit_bytes=VMEM_LIMIT_BYTES),
        name="norm",
    )(x2, g3)


def _inproj_kernel(h_ref, w_ref, o_ref):
    acc = jnp.dot(h_ref[...], w_ref[...].astype(BF16), preferred_element_type=F32)
    for c in range(o_ref.shape[0]):
        o_ref[c] = acc[:, c * LANES:(c + 1) * LANES]


def _inproj(h, w, layer=None, tm=2048):
    t, d = h.shape
    n = w.shape[-1]
    if w.ndim == 2:
        tn = 1024
        w_spec = pl.BlockSpec((d, tn), lambda i, j: (0, j))
    else:
        tn = 512
        w_spec = pl.BlockSpec((None, d, tn), lambda i, j: (layer, 0, j))
    return pl.pallas_call(
        _inproj_kernel,
        grid=(t // tm, n // tn),
        in_specs=[
            pl.BlockSpec((tm, d), lambda i, j: (i, 0)),
            w_spec,
        ],
        out_specs=pl.BlockSpec((tn // LANES, tm, LANES), lambda i, j: (j, i, 0)),
        out_shape=jax.ShapeDtypeStruct((n // LANES, t, LANES), F32),
        compiler_params=pltpu.CompilerParams(
            dimension_semantics=("parallel", "arbitrary"),
            vmem_limit_bytes=VMEM_LIMIT_BYTES),
        name="inproj",
    )(h, w)


def _hgrn_level_widths():
    w = HG_CHUNK // 2
    widths = []
    while w >= 1:
        widths.append(w)
        w //= 2
    return widths


def _hgrn_masks():
    t = np.arange(HG_CHUNK)[:, None]
    s = np.arange(HG_CHUNK)[None, :]
    masks = [((t // (2 * w)) == (s // (2 * w))) & ((t % (2 * w)) >= w) & ((s % (2 * w)) < w)
             for w in _hgrn_level_widths()]
    return np.stack(masks).astype(np.float32)


_NT = (((1,), (1,)), ((), ()))
_TN = (((0,), (0,)), ((), ()))


def _hgrn_head_steps(q_ref, f_ref, v_ref, gate_ref, lbp, ng, masks_ref, o_ref, layer):
    c = HG_CHUNK
    nt = c // SUBLANES
    widths = _hgrn_level_widths()
    n_chunks = q_ref.shape[0] // c
    lbe = jnp.exp(lbp - jnp.max(lbp, axis=0, keepdims=True))
    sm = lbe / jnp.sum(lbe, axis=0, keepdims=True)
    lb = jnp.sum(sm[1:layer + 1], axis=0, keepdims=True) if layer else jnp.zeros((1, HG_DK), F32)
    one_minus_lb = 1.0 - lb

    row8 = lax.broadcasted_iota(jnp.int32, (SUBLANES, LANES), 0)
    scan_masks = [(k, (row8 >= k).astype(F32)) for k in (1, 2, 4)]
    second = {w: (row8 & w) != 0 for w in (4, 2, 1)}
    sign = {w: jnp.where(second[w], 1.0, -1.0) for w in (4, 2)}
    low4 = row8 < 4

    def rows(idx):
        return slice(idx * c, (idx + 1) * c)

    def front(idx):
        sl = rows(idx)
        q = q_ref[sl, :]
        v = v_ref[sl, :]
        f = lb + one_minus_lb / (1.0 + jnp.exp(-f_ref[sl, :]))
        kk = 1.0 - f
        g = jnp.log2(f)

        cum, ends = [], []
        for i in range(nt):
            t = _tile(g, i)
            for k, m in scan_masks:
                t = t + pltpu.roll(t, k, 0) * m
            if i:
                t = t + ends[-1]
            cum.append(t)
            ends.append(_bcast_row(t, SUBLANES - 1))

        xs = []
        for w in widths:
            tiles = []
            for i in range(nt):
                qt, kt, ct = _tile(q, i), _tile(kk, i), cum[i]
                if w >= SUBLANES:
                    r = i * SUBLANES
                    mid = ends[(r - r % (2 * w) + w) // SUBLANES - 1]
                    x = qt * jnp.exp2(ct - mid) if r & w else kt * jnp.exp2(mid - ct)
                elif w == 4:
                    x = jnp.where(second[4], qt, kt) * jnp.exp2((ct - _bcast_row(ct, 3)) * sign[4])
                elif w == 2:
                    mid = jnp.where(low4, _bcast_row(ct, 1), _bcast_row(ct, 5))
                    x = jnp.where(second[2], qt, kt) * jnp.exp2((ct - mid) * sign[2])
                else:
                    x = jnp.where(second[1], qt * _tile(f, i), kt)
                tiles.append(x)
            xs.append(jnp.concatenate(tiles, axis=0).astype(BF16))

        last = ends[-1]
        qi = jnp.concatenate([_tile(q, i) * jnp.exp2(cum[i]) for i in range(nt)], axis=0)
        ks = jnp.concatenate([_tile(kk, i) * jnp.exp2(last - cum[i]) for i in range(nt)], axis=0)
        dec = jnp.exp2(last)
        od = jnp.sum(q * kk, axis=-1, keepdims=True) * v
        return xs, qi.astype(BF16), ks.astype(BF16), v.astype(BF16), dec, od

    def mid_start(fr, st):
        xs, qi, ks, vb, dec, od = fr
        a = [lax.dot_general(x, x, _NT, preferred_element_type=F32) for x in xs]
        upd = lax.dot_general(vb, ks, _TN, preferred_element_type=F32)
        o_inter = lax.dot_general(qi, st.astype(BF16), _NT, preferred_element_type=F32)
        return a, upd, o_inter

    def mid_finish(fr, st, a, upd, o_inter):
        _, _, _, vb, dec, od = fr
        tiles = []
        for i in range(nt):
            r = i * SUBLANES
            acc = None
            for lvl, w in enumerate(widths):
                if w >= SUBLANES and not r & w:
                    continue
                term = _tile(a[lvl], i) * masks_ref[lvl, r:r + SUBLANES, :]
                acc = term if acc is None else acc + term
            tiles.append(acc)
        scores = jnp.concatenate(tiles, axis=0).astype(BF16)
        return (scores, o_inter + od, vb), st * dec[0:1, :] + upd

    def back_start(md):
        scores, o_partial, vb = md
        return o_partial + jnp.dot(scores, vb, preferred_element_type=F32)

    def back_finish(o, idx):
        ms = jnp.mean(o * o, axis=-1, keepdims=True)
        y = o * lax.rsqrt(ms + EPS) * ng * _silu(gate_ref[rows(idx), :])
        o_ref[rows(idx), :] = y.astype(o_ref.dtype)

    st = jnp.zeros((HG_DV, HG_DK), F32)
    fr = front(0)
    md = None
    for i in range(n_chunks):
        started = mid_start(fr, st)
        o_back = back_start(md) if md is not None else None
        nxt = front(i + 1) if i + 1 < n_chunks else None
        if o_back is not None:
            back_finish(o_back, i - 1)
        md, st = mid_finish(fr, st, *started)
        fr = nxt
        yield
    back_finish(back_start(md), n_chunks - 1)


def _hgrn_kernel(*refs, layer, side_cast):
    if side_cast:
        q_ref, f_ref, v_ref, gate_ref, lbp_ref, ng_ref, masks_ref, wi_ref, o_ref, wib_ref = refs
        wib_ref[...] = wi_ref[...].astype(BF16)
    else:
        q_ref, f_ref, v_ref, gate_ref, lbp_ref, ng_ref, masks_ref, o_ref = refs
    lbp = lbp_ref[...]
    ng = ng_ref[...]
    heads = []
    for hh in range(q_ref.shape[0]):
        lanes = slice(hh * LANES, (hh + 1) * LANES)
        heads.append(_hgrn_head_steps(
            q_ref.at[hh], f_ref.at[hh], v_ref.at[hh], gate_ref.at[hh], lbp[:, lanes], ng[:, lanes],
            masks_ref, o_ref.at[:, lanes], layer))
    for _ in itertools.zip_longest(*heads):
        pass


def _hgrn(proj, lb_param, ng, layer, batch, seq, w_in=None, cast_layer=None):
    masks = _hgrn_masks()
    t = proj.shape[1]
    nh = HG_HEADS
    hps = HG_HEADS_PER_STEP
    steps = nh // hps
    depth = lb_param.shape[0]

    def col(base):
        return pl.BlockSpec((hps, seq, LANES), lambda b, h: (base // hps + h, b, 0))

    in_specs = [
        col(0), col(nh), col(2 * nh), col(3 * nh),
        pl.BlockSpec((depth, hps * LANES), lambda b, h: (0, h)),
        pl.BlockSpec((None, 1, hps * LANES), lambda b, h: (layer, 0, h)),
        pl.BlockSpec(masks.shape, lambda b, h: (0, 0, 0)),
    ]
    out_specs = [pl.BlockSpec((seq, hps * LANES), lambda b, h: (b, h))]
    out_shape = [jax.ShapeDtypeStruct((t, nh * HG_DV), BF16)]
    args = [proj, proj, proj, proj, lb_param, ng, jnp.asarray(masks)]
    if w_in is not None:
        d, n = w_in.shape[1:]
        rows = d // (batch * steps)
        in_specs.append(pl.BlockSpec((None, rows, n), lambda b, h: (cast_layer, b * steps + h, 0)))
        out_specs.append(pl.BlockSpec((rows, n), lambda b, h: (b * steps + h, 0)))
        out_shape.append(jax.ShapeDtypeStruct((d, n), BF16))
        args.append(w_in)
    outs = pl.pallas_call(
        functools.partial(_hgrn_kernel, layer=layer, side_cast=w_in is not None),
        grid=(batch, steps),
        in_specs=in_specs,
        out_specs=out_specs,
        out_shape=out_shape,
        compiler_params=pltpu.CompilerParams(
            dimension_semantics=("parallel", "parallel"),
            vmem_limit_bytes=VMEM_LIMIT_BYTES),
        name="hgrn2",
    )(*args)
    return outs if w_in is not None else outs[0]


def _lru_block_step(x_ref, gate_ref, cw, cb, wri_ref, bri, lam, o_ref):
    rws = LRU_ROWS
    nt = rws // SUBLANES
    nla_half = (0.5 * LRU_C) * (jnp.log1p(jnp.exp(-jnp.abs(lam))) + jnp.maximum(-lam, 0.0))
    tap = [cw[CONV_WIDTH - 1 - s:CONV_WIDTH - s, :] for s in range(CONV_WIDTH)]
    row8 = lax.broadcasted_iota(jnp.int32, (SUBLANES, LANES), 0)
    from_prev = [row8 < s for s in range(CONV_WIDTH)]
    scan_masks = [(k, row8 >= k) for k in (1, 2, 4)]

    def step(n, carry):
        prev8, hlast = carry
        r0 = pl.multiple_of(n * rws, rws)
        x = x_ref[pl.ds(r0, rws), :]

        prev_rolled = [pltpu.roll(prev8, s, 0) for s in range(1, CONV_WIDTH)]
        xc_tiles = []
        for j in range(nt):
            t = _tile(x, j)
            rolled = [pltpu.roll(t, s, 0) for s in range(1, CONV_WIDTH)]
            acc = t * tap[0] + cb
            for s in range(1, CONV_WIDTH):
                acc = acc + jnp.where(from_prev[s], prev_rolled[s - 1], rolled[s - 1]) * tap[s]
            prev_rolled = rolled
            xc_tiles.append(acc)
        xc = jnp.concatenate(xc_tiles, axis=0)

        th = jnp.tanh(jnp.dot(xc.astype(BF16), wri_ref[...], preferred_element_type=F32) + bri)
        nla = nla_half + nla_half * th[:, :LANES]
        a = jnp.exp2(nla * (-1.0 / math.log(2.0)))
        one_minus_a2 = jnp.tanh(nla) * (a * a + 1.0)
        root = jnp.where(one_minus_a2 > 0.0, one_minus_a2 * lax.rsqrt(one_minus_a2), 0.0)
        half_xc = 0.5 * xc
        u = root * (half_xc + half_xc * th[:, LANES:])

        hs = []
        for j in range(nt):
            at, ut = _tile(a, j), _tile(u, j)
            for k, m in scan_masks:
                ut = ut + at * jnp.where(m, pltpu.roll(ut, k, 0), 0.0)
                at = at * jnp.where(m, pltpu.roll(at, k, 0), 1.0)
            h8 = at * hlast + ut
            hlast = _bcast_row(h8, SUBLANES - 1)
            hs.append(h8)
        h = jnp.concatenate(hs, axis=0)

        y = h * _silu(gate_ref[pl.ds(r0, rws), :])
        o_ref[pl.ds(r0, rws), :] = y.astype(o_ref.dtype)
        return _tile(x, nt - 1), hlast

    return step


def _lru_kernel(x_ref, gate_ref, cw_ref, cb_ref, wri_ref, bri_ref, lam_ref, wo_ref, o_ref, wob_ref):
    wob_ref[...] = wo_ref[...].astype(BF16)
    cw, cb, bri, lam = cw_ref[...], cb_ref[...], bri_ref[...], lam_ref[...]
    steps = []
    for k in range(x_ref.shape[0]):
        lanes = slice(k * LANES, (k + 1) * LANES)
        lanes2 = slice(2 * k * LANES, 2 * (k + 1) * LANES)
        steps.append(_lru_block_step(x_ref.at[k], gate_ref.at[k], cw[:, lanes], cb[:, lanes],
                                     wri_ref.at[k], bri[:, lanes2], lam[:, lanes],
                                     o_ref.at[:, lanes]))

    def body(n, carry):
        return tuple(step(n, c) for step, c in zip(steps, carry))

    zeros8 = jnp.zeros((SUBLANES, LANES), F32)
    lax.fori_loop(0, x_ref.shape[1] // LRU_ROWS, body, tuple((zeros8, zeros8) for _ in steps))


def _lru(proj, cw, cb, wri, bri, lam, w_out, layer, batch, seq, x_col, gate_col):
    t = proj.shape[1]
    nb = LRU_BLOCKS
    bps = LRU_BLOCKS_PER_STEP
    steps = nb // bps
    k, d = w_out.shape[1:]
    wo_rows = k // (batch * steps)
    wo_spec = pl.BlockSpec((None, wo_rows, d), lambda b, n: (layer, b * steps + n, 0))
    wob_spec = pl.BlockSpec((wo_rows, d), lambda b, n: (b * steps + n, 0))

    def col(base):
        return pl.BlockSpec((bps, seq, LANES), lambda b, n: (base // bps + n, b, 0))

    def vec(rows, width=LANES):
        return pl.BlockSpec((None, rows, bps * width), lambda b, n: (layer, 0, n))

    return pl.pallas_call(
        _lru_kernel,
        grid=(batch, steps),
        in_specs=[
            col(x_col), col(gate_col),
            vec(CONV_WIDTH), vec(1),
            pl.BlockSpec((None, bps, LRU_BLOCK, 2 * LRU_BLOCK), lambda b, n: (layer, n, 0, 0)),
            vec(1, 2 * LANES),
            vec(1),
            wo_spec,
        ],
        out_specs=[pl.BlockSpec((seq, bps * LANES), lambda b, n: (b, n)), wob_spec],
        out_shape=[jax.ShapeDtypeStruct((t, nb * LRU_BLOCK), BF16),
                   jax.ShapeDtypeStruct((k, d), BF16)],
        compiler_params=pltpu.CompilerParams(
            dimension_semantics=("parallel", "parallel"),
            vmem_limit_bytes=VMEM_LIMIT_BYTES),
        name="rglru",
    )(proj, proj, cw, cb, wri, bri, lam, w_out)


def _kv_kernel(mem_ref, g_ref, w_ref, o_ref, h_ref):
    @pl.when(pl.program_id(1) == 0)
    def _():
        h_ref[...] = _rmsnorm_rows(mem_ref[...], g_ref[...]).astype(BF16)

    o_ref[...] = jnp.dot(h_ref[...], w_ref[...].astype(BF16),
                         preferred_element_type=F32).astype(o_ref.dtype)


def _kv(mem2, g, w_kv, tn=512):
    depth, d, n = w_kv.shape
    m = mem2.shape[0]
    return pl.pallas_call(
        _kv_kernel,
        grid=(depth, n // tn),
        in_specs=[
            pl.BlockSpec((m, d), lambda l, j: (0, 0)),
            pl.BlockSpec((None, 1, d), lambda l, j: (l, 0, 0)),
            pl.BlockSpec((None, d, tn), lambda l, j: (l, 0, j)),
        ],
        out_specs=pl.BlockSpec((None, m, tn), lambda l, j: (l, 0, j)),
        out_shape=jax.ShapeDtypeStruct((depth, m, n), BF16),
        scratch_shapes=[pltpu.VMEM((m, d), BF16)],
        compiler_params=pltpu.CompilerParams(
            dimension_semantics=("parallel", "arbitrary"),
            vmem_limit_bytes=VMEM_LIMIT_BYTES),
        name="mem_kv",
    )(mem2, g, w_kv)


def _attn_kernel(q_ref, gate_ref, k_ref, v_ref, o_ref):
    scale = XA_HEAD_DIM ** -0.5
    per_head = XA_HEAD_DIM // LANES

    def head_cols(ref, h):
        return jnp.concatenate([ref[h * per_head + c] for c in range(per_head)], axis=1)

    for h in range(XA_HEADS):
        sl = slice(h * XA_HEAD_DIM, (h + 1) * XA_HEAD_DIM)
        s = lax.dot_general(head_cols(q_ref, h).astype(BF16), k_ref[:, sl], _NT,
                            preferred_element_type=F32) * scale
        m = jnp.max(s, axis=-1, keepdims=True)
        p = jnp.exp(s - m)
        p = p / jnp.sum(p, axis=-1, keepdims=True)
        o = jnp.dot(p.astype(BF16), v_ref[:, sl], preferred_element_type=F32)
        o_ref[:, sl] = (o * _silu(head_cols(gate_ref, h))).astype(o_ref.dtype)


def _attn(proj, kv, layer, batch, seq, n_mem, q_col, gate_col, ts=1024):
    t = proj.shape[1]
    width = XA_HEADS * XA_HEAD_DIM
    nt = seq // ts
    cols = width // LANES
    return pl.pallas_call(
        _attn_kernel,
        grid=(batch, nt),
        in_specs=[
            pl.BlockSpec((cols, ts, LANES), lambda b, i: (q_col, b * nt + i, 0)),
            pl.BlockSpec((cols, ts, LANES), lambda b, i: (gate_col, b * nt + i, 0)),
            pl.BlockSpec((None, n_mem, width), lambda b, i: (layer * batch + b, 0, 0)),
            pl.BlockSpec((None, n_mem, width), lambda b, i: (layer * batch + b, 0, 1)),
        ],
        out_specs=pl.BlockSpec((ts, width), lambda b, i: (b * nt + i, 0)),
        out_shape=jax.ShapeDtypeStruct((t, width), BF16),
        compiler_params=pltpu.CompilerParams(
            dimension_semantics=("parallel", "parallel"),
            vmem_limit_bytes=VMEM_LIMIT_BYTES),
        name="mem_attn",
    )(proj, proj, kv, kv)


def _outproj_kernel(x_ref, a_ref, b_ref, c_ref, w_ref, g_ref, *out_refs):
    mixed = jnp.concatenate([a_ref[...], b_ref[...], c_ref[...]], axis=1)
    acc = x_ref[...] + jnp.dot(mixed, w_ref[...], preferred_element_type=F32)
    y_ref = out_refs[-1]
    y_ref[...] = _rmsnorm_rows(acc, g_ref[...]).astype(y_ref.dtype)
    if len(out_refs) == 2:
        out_refs[0][...] = acc


def _outproj(x2, oa, ob, oc, w_bf16, layer, g3, last, tm=512):
    t, d = x2.shape
    k = w_bf16.shape[0]

    def lhs(arr):
        return pl.BlockSpec((tm, arr.shape[1]), lambda i: (i, 0))

    row_block = pl.BlockSpec((tm, d), lambda i: (i, 0))
    if last:
        out_specs = [row_block]
        out_shape = [jax.ShapeDtypeStruct((t, d), F32)]
    else:
        out_specs = [row_block, row_block]
        out_shape = [jax.ShapeDtypeStruct((t, d), F32), jax.ShapeDtypeStruct((t, d), BF16)]
    return pl.pallas_call(
        _outproj_kernel,
        grid=(t // tm,),
        in_specs=[
            row_block,
            lhs(oa), lhs(ob), lhs(oc),
            pl.BlockSpec((k, d), lambda i: (0, 0), pipeline_mode=pl.Buffered(1)),
            pl.BlockSpec((None, 1, d), lambda i: (layer, 0, 0)),
        ],
        out_specs=out_specs,
        out_shape=out_shape,
        compiler_params=pltpu.CompilerParams(
            dimension_semantics=("parallel",),
            vmem_limit_bytes=VMEM_LIMIT_BYTES),
        name="outproj",
    )(x2, oa, ob, oc, w_bf16, g3)


def kernel(x, mem, norm_g, w_in, lb_param, hg_norm_g, conv_w, conv_b, w_r, b_r, w_i, b_i, lam,
           mem_norm_g, w_kv, w_out, final_g):
    batch, seq, d_model = x.shape
    depth = w_in.shape[0]
    n_mem = mem.shape[1]
    hg_key = HG_HEADS * HG_DK
    hg_val = HG_HEADS * HG_DV
    lru_w = LRU_BLOCKS * LRU_BLOCK
    xa_w = XA_HEADS * XA_HEAD_DIM
    assert seq % HG_CHUNK == 0 and seq % LRU_ROWS == 0
    assert w_in.shape[2] == 2 * hg_key + 2 * hg_val + 2 * lru_w + 2 * xa_w
    assert hg_key == hg_val == lru_w == xa_w
    lru_x_col = (2 * hg_key + 2 * hg_val) // LANES
    lru_gate_col = lru_x_col + lru_w // LANES
    xa_q_col = (2 * hg_key + 2 * hg_val + 2 * lru_w) // xa_w
    xa_gate_col = xa_q_col + 1

    wri = (0.5 * jnp.concatenate([w_r, w_i], axis=-1)).astype(BF16)
    bri = (0.5 * jnp.concatenate([b_r, b_i], axis=-1)).reshape(depth, 1, -1)
    post_g3 = jnp.concatenate([norm_g[1:], final_g[None]], axis=0).reshape(depth, 1, d_model)
    norm_g3 = norm_g.reshape(depth, 1, d_model)
    hg_norm_g3 = hg_norm_g.reshape(depth, 1, hg_val)
    conv_b3 = conv_b.reshape(depth, 1, lru_w)
    lam3 = lam.reshape(depth, 1, lru_w)

    x2 = x.reshape(batch * seq, d_model)
    kv = _kv(mem.reshape(batch * n_mem, d_model), mem_norm_g.reshape(depth, 1, d_model), w_kv)
    kv = kv.reshape(depth * batch, n_mem, 2 * xa_w)

    h = _norm(x2, norm_g3, 0, BF16)
    w_in_b = None
    for l in range(depth):
        proj = _inproj(h, w_in, l) if w_in_b is None else _inproj(h, w_in_b)
        if l + 1 < depth:
            o_a, w_in_b = _hgrn(proj, lb_param, hg_norm_g3, l, batch, seq, w_in, l + 1)
        else:
            o_a = _hgrn(proj, lb_param, hg_norm_g3, l, batch, seq)
        o_b, w_out_b = _lru(proj, conv_w, conv_b3, wri, bri, lam3, w_out, l, batch, seq,
                            lru_x_col, lru_gate_col)
        o_c = _attn(proj, kv, l, batch, seq, n_mem, xa_q_col, xa_gate_col)
        outs = _outproj(x2, o_a, o_b, o_c, w_out_b, l, post_g3, l + 1 == depth)
        if l + 1 < depth:
            x2, h = outs
    return outs[0].reshape(batch, seq, d_model)
```
